```python
import math
import jax
import jax.numpy as jnp
from jax import lax

D_MODEL = 1024
BATCH = 32
SEQ = 2048
DEPTH = 4

N_MIXERS = 4
HEAD_DIM = 64
BLOCK = 128
RMS_EPS = 1e-6
NUM_BUCKETS = 32
MAX_DISTANCE = 2048
N_BIAS_HEADS = 16
DIL_GROUPS = ((128, 1), (512, 4), (2048, 16))
N_DIL = 3
A_HEADS = 8
MLA_HEADS = 16
MLA_NOPE = 64
MLA_ROPE = 32
MLA_QK = 96
MLA_V = 64
MLA_Q_RANK = 384
MLA_KV_RANK = 256
ROPE_THETA = 10000.0
DIFF_HEADS = 8
SWA_Q_HEADS = 16
SWA_KV_HEADS = 2
SWA_WINDOW = 128
D_FF = 2816
CONV_WIDTH = 3
N_A = (DEPTH + 3) // 4
N_B = (DEPTH + 2) // 4
N_C = (DEPTH + 1) // 4
N_D = DEPTH // 4

kernel_name = "hybrid_interleaved_dilated_mla_diff_swa_convffn"


def rms_norm(x, g):
    xf = x.astype(jnp.float32)
    y = xf * lax.rsqrt(jnp.mean(xf * xf, axis=-1, keepdims=True) + RMS_EPS)
    return (y * g.astype(jnp.float32)).astype(x.dtype)


def t5_bucket(dist):
    max_exact = NUM_BUCKETS // 2
    d_f = jnp.maximum(dist, 1).astype(jnp.float32)
    large = max_exact + (jnp.log(d_f / max_exact) / math.log(MAX_DISTANCE / max_exact)
                         * (NUM_BUCKETS - max_exact)).astype(jnp.int32)
    return jnp.where(dist < max_exact, dist, jnp.minimum(large, NUM_BUCKETS - 1))


def band_bias(table_cols, dilation):
    offset = jnp.arange(BLOCK)[:, None] + BLOCK - jnp.arange(2 * BLOCK)[None, :]
    bucket = t5_bucket(jnp.maximum(offset, 0) * dilation)
    return table_cols.T[:, bucket].astype(jnp.float32)


def banded_attention(q, k, v, bias, window, sinks=None):
    n, L, hq, dh = q.shape
    hk = k.shape[2]
    g = hq // hk
    nb = -(-L // BLOCK)
    pad = nb * BLOCK - L
    padl = lambda a: jnp.pad(a, ((0, 0), (0, pad), (0, 0), (0, 0)))
    q, k, v = padl(q), padl(k), padl(v)
    qb = q.reshape(n, nb, BLOCK, hk, g, dh)

    def pairs(a):
        cur = a.reshape(n, nb, BLOCK, hk, dh)
        prev = jnp.pad(cur, ((0, 0), (1, 0), (0, 0), (0, 0), (0, 0)))[:, :-1]
        return jnp.concatenate([prev, cur], axis=2)

    kb, vb = pairs(k), pairs(v)
    kj = jnp.arange(2 * BLOCK)
    offset = jnp.arange(BLOCK)[:, None] + BLOCK - kj[None, :]
    key_pos = jnp.arange(nb)[:, None, None] * BLOCK + kj[None, None, :] - BLOCK
    mask = (offset >= 0) & (offset <= window) & (key_pos >= 0)
    logits = jnp.einsum('nbqhgd,nbkhd->nbhgqk', qb, kb, preferred_element_type=jnp.float32) * (dh ** -0.5)
    logits = logits + bias.reshape(hk, g, BLOCK, 2 * BLOCK)
    logits = jnp.where(mask[None, :, None, None], logits, -jnp.inf)
    m = jnp.max(logits, axis=-1)
    if sinks is not None:
        sk = sinks.astype(jnp.float32).reshape(hk, g, 1)
        m = jnp.maximum(m, sk)
    p = jnp.exp(logits - m[..., None])
    s = jnp.sum(p, axis=-1)
    if sinks is not None:
        s = s + jnp.exp(sk - m)
    o = jnp.einsum('nbhgqk,nbkhd->nbqhgd', p, vb.astype(jnp.float32))
    o = o / s.transpose(0, 1, 4, 2, 3)[..., None]
    o = o.reshape(n, nb * BLOCK, hq, dh)[:, :L]
    lse = (m + jnp.log(s)).transpose(0, 1, 4, 2, 3).reshape(n, nb * BLOCK, hq)[:, :L]
    return o, lse


def dilated_attention(h, w_in, q_norm, k_norm, w_out, table):
    b, s, _ = h.shape
    qkv = (h @ w_in).reshape(b, s, N_DIL, 3, A_HEADS, HEAD_DIM)
    outs, lses = [], []
    for gi, (window, dil) in enumerate(DIL_GROUPS):
        q = rms_norm(qkv[:, :, gi, 0], q_norm[gi])
        k = rms_norm(qkv[:, :, gi, 1], k_norm[gi])
        v = qkv[:, :, gi, 2]
        to_res = lambda a: a.reshape(b, s // dil, dil, A_HEADS, HEAD_DIM).transpose(0, 2, 1, 3, 4).reshape(
            b * dil, s // dil, A_HEADS, HEAD_DIM)
        o, lse = banded_attention(to_res(q), to_res(k), to_res(v), band_bias(table[:, :A_HEADS], dil), window // dil)
        outs.append(o.reshape(b, dil, s // dil, A_HEADS, HEAD_DIM).transpose(0, 2, 1, 3, 4).reshape(
            b, s, A_HEADS, HEAD_DIM))
        lses.append(lse.reshape(b, dil, s // dil, A_HEADS).transpose(0, 2, 1, 3).reshape(b, s, A_HEADS))
    alpha = jax.nn.softmax(jnp.stack(lses, axis=0), axis=0)
    o = jnp.sum(alpha[..., None] * jnp.stack(outs, axis=0), axis=0)
    return o.reshape(b, s, A_HEADS * HEAD_DIM).astype(h.dtype) @ w_out


def apply_rope(x, s):
    inv_freq = ROPE_THETA ** (-jnp.arange(0, MLA_ROPE, 2, dtype=jnp.float32) / MLA_ROPE)
    ang = jnp.arange(s, dtype=jnp.float32)[:, None] * inv_freq[None, :]
    cos, sin = jnp.cos(ang)[:, None, :], jnp.sin(ang)[:, None, :]
    xf = x.astype(jnp.float32)
    x1, x2 = xf[..., :MLA_ROPE // 2], xf[..., MLA_ROPE // 2:]
    return jnp.concatenate([x1 * cos - x2 * sin, x2 * cos + x1 * sin], axis=-1).astype(x.dtype)


def mla_attention(h, w_in, q_a_norm, kv_a_norm, w_q_up, w_kv_up, q_norm, k_norm, w_out):
    b, s, _ = h.shape
    lat = h @ w_in
    c_q = lat[..., :MLA_Q_RANK]
    c_kv = lat[..., MLA_Q_RANK:MLA_Q_RANK + MLA_KV_RANK]
    k_pe = lat[..., MLA_Q_RANK + MLA_KV_RANK:]
    q = (rms_norm(c_q, q_a_norm) @ w_q_up).reshape(b, s, MLA_HEADS, MLA_QK)
    kv = (rms_norm(c_kv, kv_a_norm) @ w_kv_up).reshape(b, s, MLA_HEADS, MLA_NOPE + MLA_V)
    v = kv[..., MLA_NOPE:]
    k = jnp.concatenate([kv[..., :MLA_NOPE], jnp.broadcast_to(k_pe[:, :, None, :], (b, s, MLA_HEADS, MLA_ROPE))], axis=-1)
    q = rms_norm(q, q_norm)
    k = rms_norm(k, k_norm)
    q = jnp.concatenate([q[..., :MLA_NOPE], apply_rope(q[..., MLA_NOPE:], s)], axis=-1)
    k = jnp.concatenate([k[..., :MLA_NOPE], apply_rope(k[..., MLA_NOPE:], s)], axis=-1)
    nb = s // BLOCK
    qb = q.reshape(b, nb, BLOCK, MLA_HEADS, MLA_QK).transpose(1, 0, 2, 3, 4)
    kpos = jnp.arange(s)
    vf = v.astype(jnp.float32)

    def block(args):
        q_blk, i = args
        qpos = i * BLOCK + jnp.arange(BLOCK)
        logits = jnp.einsum('bqhd,bkhd->bhqk', q_blk, k, preferred_element_type=jnp.float32) * (MLA_QK ** -0.5)
        logits = jnp.where(kpos[None, :] <= qpos[:, None], logits, -jnp.inf)
        p = jax.nn.softmax(logits, axis=-1)
        return jnp.einsum('bhqk,bkhd->bqhd', p, vf)

    o = lax.map(block, (qb, jnp.arange(nb)))
    o = o.transpose(1, 0, 2, 3, 4).reshape(b, s, MLA_HEADS * MLA_V)
    return o.astype(h.dtype) @ w_out


def diff_attention(h, w_in, q_norm, k_norm, lq1, lk1, lq2, lk2, subln, w_out, table, layer_idx):
    b, s, _ = h.shape
    qk_w = DIFF_HEADS * 2 * HEAD_DIM
    proj = h @ w_in
    q = rms_norm(proj[..., :qk_w].reshape(b, s, DIFF_HEADS, 2, HEAD_DIM), q_norm)
    k = rms_norm(proj[..., qk_w:2 * qk_w].reshape(b, s, DIFF_HEADS, 2, HEAD_DIM), k_norm)
    vf = proj[..., 2 * qk_w:].reshape(b, s, DIFF_HEADS, 2 * HEAD_DIM).astype(jnp.float32)
    lam_init = 0.8 - 0.6 * math.exp(-0.3 * layer_idx)
    f32 = lambda a: a.astype(jnp.float32)
    lam = jnp.exp(jnp.sum(f32(lq1) * f32(lk1))) - jnp.exp(jnp.sum(f32(lq2) * f32(lk2))) + lam_init
    tab = table[:, :2 * DIFF_HEADS].T.reshape(2, DIFF_HEADS, NUM_BUCKETS).transpose(1, 0, 2).astype(jnp.float32)
    nb = s // BLOCK
    qb = q.reshape(b, nb, BLOCK, DIFF_HEADS, 2, HEAD_DIM).transpose(1, 0, 2, 3, 4, 5)
    kpos = jnp.arange(s)

    def block(args):
        q_blk, i = args
        qpos = i * BLOCK + jnp.arange(BLOCK)
        dist = qpos[:, None] - kpos[None, :]
        bias = tab[:, :, t5_bucket(jnp.maximum(dist, 0))]
        logits = jnp.einsum('bqhmd,bkhmd->bhmqk', q_blk, k, preferred_element_type=jnp.float32) * (HEAD_DIM ** -0.5)
        logits = jnp.where(dist >= 0, logits + bias, -jnp.inf)
        p = jax.nn.softmax(logits, axis=-1)
        a = p[:, :, 0] - lam * p[:, :, 1]
        return jnp.einsum('bhqk,bkhe->bqhe', a, vf)

    o = lax.map(block, (qb, jnp.arange(nb)))
    o = o.transpose(1, 0, 2, 3, 4).reshape(b, s, DIFF_HEADS, 2 * HEAD_DIM)
    o = rms_norm(o, subln) * (1.0 - lam_init)
    return o.reshape(b, s, DIFF_HEADS * 2 * HEAD_DIM).astype(h.dtype) @ w_out


def swa_sink_attention(h, w_in, q_norm, k_norm, sinks, w_out, table):
    b, s, _ = h.shape
    qw, kw = SWA_Q_HEADS * HEAD_DIM, SWA_KV_HEADS * HEAD_DIM
    proj = h @ w_in
    q = rms_norm(proj[..., :qw].reshape(b, s, SWA_Q_HEADS, HEAD_DIM), q_norm)
    k = rms_norm(proj[..., qw:qw + kw].reshape(b, s, SWA_KV_HEADS, HEAD_DIM), k_norm)
    v = proj[..., qw + kw:].reshape(b, s, SWA_KV_HEADS, HEAD_DIM)
    o, _ = banded_attention(q, k, v, band_bias(table[:, :SWA_Q_HEADS], 1), SWA_WINDOW - 1, sinks)
    return o.reshape(b, s, qw).astype(h.dtype) @ w_out


def conv_ffn(h, w_up, conv_w, conv_b, w_down):
    s = h.shape[1]
    gu = h @ w_up
    gate, up = gu[..., :D_FF], gu[..., D_FF:]
    gp = jnp.pad(gate, ((0, 0), (CONV_WIDTH - 1, 0), (0, 0)))
    conv = conv_b + conv_w[CONV_WIDTH - 1] * gate
    for j in range(CONV_WIDTH - 1):
        conv = conv + conv_w[j] * gp[:, j:j + s]
    return (jax.nn.silu(conv) * up) @ w_down


def setup_inputs(seed: int = 0) -> dict:
    key = jax.random.key(seed)
    ks = iter(jax.random.split(key, 40))
    nrm = lambda shape, scale: jax.random.normal(next(ks), shape, jnp.float32) * scale
    w = lambda shape: nrm(shape, shape[-2] ** -0.5)
    gain = lambda shape: 1.0 + nrm(shape, 0.02)
    return {
        "x": nrm((BATCH, SEQ, D_MODEL), 1.0),
        "rel_bias_table": nrm((NUM_BUCKETS, N_BIAS_HEADS), 0.2),
        "norm_mix": gain((DEPTH, D_MODEL)),
        "norm_ffn": gain((DEPTH, D_MODEL)),
        "a_w_in": w((N_A, D_MODEL, N_DIL * 3 * A_HEADS * HEAD_DIM)),
        "a_q_norm": gain((N_A, N_DIL, HEAD_DIM)),
        "a_k_norm": gain((N_A, N_DIL, HEAD_DIM)),
        "a_w_out": w((N_A, A_HEADS * HEAD_DIM, D_MODEL)),
        "b_w_in": w((N_B, D_MODEL, MLA_Q_RANK + MLA_KV_RANK + MLA_ROPE)),
        "b_q_a_norm": gain((N_B, MLA_Q_RANK)),
        "b_kv_a_norm": gain((N_B, MLA_KV_RANK)),
        "b_w_q_up": w((N_B, MLA_Q_RANK, MLA_HEADS * MLA_QK)),
        "b_w_kv_up": w((N_B, MLA_KV_RANK, MLA_HEADS * (MLA_NOPE + MLA_V))),
        "b_q_norm": gain((N_B, MLA_QK)),
        "b_k_norm": gain((N_B, MLA_QK)),
        "b_w_out": w((N_B, MLA_HEADS * MLA_V, D_MODEL)),
        "c_w_in": w((N_C, D_MODEL, DIFF_HEADS * 2 * HEAD_DIM * 3)),
        "c_q_norm": gain((N_C, HEAD_DIM)),
        "c_k_norm": gain((N_C, HEAD_DIM)),
        "c_lambda_q1": nrm((N_C, HEAD_DIM), 0.1),
        "c_lambda_k1": nrm((N_C, HEAD_DIM), 0.1),
        "c_lambda_q2": nrm((N_C, HEAD_DIM), 0.1),
        "c_lambda_k2": nrm((N_C, HEAD_DIM), 0.1),
        "c_subln": gain((N_C, 2 * HEAD_DIM)),
        "c_w_out": w((N_C, DIFF_HEADS * 2 * HEAD_DIM, D_MODEL)),
        "d_w_in": w((N_D, D_MODEL, (SWA_Q_HEADS + 2 * SWA_KV_HEADS) * HEAD_DIM)),
        "d_q_norm": gain((N_D, HEAD_DIM)),
        "d_k_norm": gain((N_D, HEAD_DIM)),
        "d_sinks": nrm((N_D, SWA_Q_HEADS), 0.5),
        "d_w_out": w((N_D, SWA_Q_HEADS * HEAD_DIM, D_MODEL)),
        "f_w_up": w((DEPTH, D_MODEL, 2 * D_FF)),
        "f_conv_w": nrm((DEPTH, CONV_WIDTH, D_FF), CONV_WIDTH ** -0.5),
        "f_conv_b": nrm((DEPTH, D_FF), 0.01),
        "f_w_down": w((DEPTH, D_FF, D_MODEL)),
    }


def reference(x, rel_bias_table, norm_mix, norm_ffn,
              a_w_in, a_q_norm, a_k_norm, a_w_out,
              b_w_in, b_q_a_norm, b_kv_a_norm, b_w_q_up, b_w_kv_up, b_q_norm, b_k_norm, b_w_out,
              c_w_in, c_q_norm, c_k_norm, c_lambda_q1, c_lambda_k1, c_lambda_q2, c_lambda_k2, c_subln, c_w_out,
              d_w_in, d_q_norm, d_k_norm, d_sinks, d_w_out,
              f_w_up, f_conv_w, f_conv_b, f_w_down):
    for i in range(DEPTH):
        m, j = i % N_MIXERS, i // N_MIXERS
        h = rms_norm(x, norm_mix[i])
        if m == 0:
            y = dilated_attention(h, a_w_in[j], a_q_norm[j], a_k_norm[j], a_w_out[j], rel_bias_table)
        elif m == 1:
            y = mla_attention(h, b_w_in[j], b_q_a_norm[j], b_kv_a_norm[j], b_w_q_up[j], b_w_kv_up[j],
                              b_q_norm[j], b_k_norm[j], b_w_out[j])
        elif m == 2:
            y = diff_attention(h, c_w_in[j], c_q_norm[j], c_k_norm[j], c_lambda_q1[j], c_lambda_k1[j],
                               c_lambda_q2[j], c_lambda_k2[j], c_subln[j], c_w_out[j], rel_bias_table, i)
        else:
            y = swa_sink_attention(h, d_w_in[j], d_q_norm[j], d_k_norm[j], d_sinks[j], d_w_out[j], rel_bias_table)
        x = x + y
        x = x + conv_ffn(rms_norm(x, norm_ffn[i]), f_w_up[i], f_conv_w[i], f_conv_b[i], f_w_down[i])
    return x
```

```python
import functools
import math

import jax
import jax.numpy as jnp
from jax import lax
from jax.experimental import pallas as pl
from jax.experimental.pallas import tpu as pltpu

F32 = jnp.float32
BF16 = jnp.bfloat16

D_MODEL = 1024
HEAD_DIM = 64
BLOCK = 128
RMS_EPS = 1e-6
NUM_BUCKETS = 32
MAX_DISTANCE = 2048
DIL_GROUPS = ((128, 1), (512, 4), (2048, 16))
N_DIL = 3
A_HEADS = 8
MLA_HEADS = 16
MLA_NOPE = 64
MLA_ROPE = 32
MLA_QK = 96
MLA_V = 64
MLA_Q_RANK = 384
MLA_KV_RANK = 256
ROPE_THETA = 10000.0
DIFF_HEADS = 8
SWA_Q_HEADS = 16
SWA_KV_HEADS = 2
SWA_WINDOW = 128
D_FF = 2816
CONV_WIDTH = 3

LANES = 128
MXU_DIM = 256
NEG = -1e30
ROW_TILE = 512
FF_CHUNK = 256
KV_CHUNK = 256
VMEM_LIMIT = 56 * 1024 * 1024


def _dot(a, b):
    return jnp.dot(a, b, preferred_element_type=F32)


def _dot_nt(a, b):
    return lax.dot_general(a, b, (((1,), (1,)), ((), ())), preferred_element_type=F32)


def _rms(x, g):
    ms = jnp.mean(x * x, axis=-1, keepdims=True)
    return x * lax.rsqrt(ms + RMS_EPS) * g


def _const_spec(shape):
    nd = len(shape)
    return pl.BlockSpec(shape, lambda *_: (0,) * nd, pipeline_mode=pl.Buffered(1))


def _params(sem):
    return pltpu.CompilerParams(dimension_semantics=sem, vmem_limit_bytes=VMEM_LIMIT)


def _lane_lo(shape):
    return lax.broadcasted_iota(jnp.int32, shape, len(shape) - 1) < HEAD_DIM


def _seg_norm(y, bd_ref):
    w = y.shape[-1]
    bd = bd_ref[:w, :w]
    y2 = y * y
    hi = y2.astype(BF16)
    lo = (y2 - hi.astype(F32)).astype(BF16)
    ms = _dot(hi, bd) + _dot(lo, bd)
    return y * lax.rsqrt(ms + RMS_EPS)


def _proj_kernel(segs, x_ref, g_ref, w_ref, gain_ref, bd_ref, *out_refs):
    hn = _rms(x_ref[...], g_ref[...]).astype(BF16)
    for (c0, width, norm, oi, oc) in segs:
        y = _dot(hn, w_ref[:, c0:c0 + width])
        if norm:
            y = _seg_norm(y, bd_ref) * gain_ref[:, c0:c0 + width]
        out_refs[oi][:, oc:oc + width] = y.astype(out_refs[oi].dtype)


def _proj(x2d, g, w, gain, bd, segs, outs):
    t = x2d.shape[0]
    n = w.shape[1]
    return pl.pallas_call(
        functools.partial(_proj_kernel, segs),
        grid=(t // ROW_TILE,),
        in_specs=[
            pl.BlockSpec((ROW_TILE, D_MODEL), lambda i: (i, 0)),
            _const_spec((1, D_MODEL)),
            _const_spec((D_MODEL, n)),
            _const_spec((1, n)),
            _const_spec((MXU_DIM, MXU_DIM)),
        ],
        out_specs=[pl.BlockSpec((ROW_TILE, ow), lambda i: (i, 0)) for ow, _ in outs],
        out_shape=[jax.ShapeDtypeStruct((t, ow), od) for ow, od in outs],
        compiler_params=_params(("parallel",)),
    )(x2d, g, w, gain, bd)


def _band_unit(q_lo, q_hi, segs, sinks=None):
    res = []
    for h, qh in enumerate((q_lo, q_hi)):
        ss = [_dot_nt(qh, k) + (b_lo, b_hi)[h] for (k, _, b_lo, b_hi) in segs]
        m = jnp.max(functools.reduce(jnp.maximum, ss), axis=-1, keepdims=True)
        if sinks is not None:
            m = jnp.maximum(m, sinks[h])
        ps = [jnp.exp(s - m) for s in ss]
        l = jnp.sum(functools.reduce(jnp.add, ps), axis=-1, keepdims=True)
        if sinks is not None:
            l = l + jnp.exp(sinks[h] - m)
        acc = sum(_dot(p.astype(BF16), v) for p, (_, v, _, _) in zip(ps, segs))
        res.append((acc, m, l))
    lo = _lane_lo((BLOCK, LANES))
    (a0, m0, l0), (a1, m1, l1) = res
    return (jnp.where(lo, a0, a1),
            jnp.where(lo, jnp.broadcast_to(m0, (BLOCK, LANES)), jnp.broadcast_to(m1, (BLOCK, LANES))),
            jnp.where(lo, jnp.broadcast_to(l0, (BLOCK, LANES)), jnp.broadcast_to(l1, (BLOCK, LANES))))


def _split_pair(q):
    lo = _lane_lo(q.shape)
    zero = jnp.zeros_like(q)
    return jnp.where(lo, q, zero).astype(BF16), jnp.where(lo, zero, q).astype(BF16)


def _dil_attn_kernel(seq, q0, k0, v0, q1, k1, v1, q2, k2, v2, bias_ref, o_ref,
                     acc_scr, m_scr, l_scr):
    nblk = seq // BLOCK

    def put(g, rows, unit):
        acc, m, l = unit
        acc_scr[g, rows, :] = acc
        m_scr[g, rows, :] = m
        l_scr[g, rows, :] = l

    def bias(g, h, lo, hi):
        return bias_ref[g, h, :, lo:hi]

    def g0_unit(qrows, krows, with_prev):
        q_lo, q_hi = _split_pair(q0[qrows, :].astype(F32))
        lo = 0 if with_prev else BLOCK
        return _band_unit(q_lo, q_hi, [(k0[krows, :], v0[krows, :],
                                        bias(0, 0, lo, 2 * BLOCK), bias(0, 1, lo, 2 * BLOCK))])

    put(0, pl.ds(0, BLOCK), g0_unit(pl.ds(0, BLOCK), pl.ds(0, BLOCK), False))

    def g0_body(j, carry):
        start = pl.multiple_of(j * BLOCK, BLOCK)
        prev = pl.multiple_of(j * BLOCK - BLOCK, BLOCK)
        put(0, pl.ds(start, BLOCK), g0_unit(pl.ds(start, BLOCK), pl.ds(prev, 2 * BLOCK), True))
        return carry

    lax.fori_loop(1, nblk, g0_body, 0)

    def strided_groups(g, dil, qr, kr, vr):
        nb = nblk // dil

        def body(r, carry):
            for j in range(nb):
                rows = pl.ds(r + dil * BLOCK * j, BLOCK, stride=dil)
                q_lo, q_hi = _split_pair(qr[rows, :])
                segs = []
                if j > 0:
                    prow = pl.ds(r + dil * BLOCK * (j - 1), BLOCK, stride=dil)
                    segs.append((kr[prow, :].astype(BF16), vr[prow, :].astype(BF16),
                                 bias(g, 0, 0, BLOCK), bias(g, 1, 0, BLOCK)))
                segs.append((kr[rows, :].astype(BF16), vr[rows, :].astype(BF16),
                             bias(g, 0, BLOCK, 2 * BLOCK), bias(g, 1, BLOCK, 2 * BLOCK)))
                put(g, rows, _band_unit(q_lo, q_hi, segs))
            return carry

        lax.fori_loop(0, dil, body, 0)

    strided_groups(1, DIL_GROUPS[1][1], q1, k1, v1)
    strided_groups(2, DIL_GROUPS[2][1], q2, k2, v2)

    def merge(c, carry):
        rows = pl.ds(pl.multiple_of(c * BLOCK, BLOCK), BLOCK)
        ms = [m_scr[g, rows, :] for g in range(N_DIL)]
        mm = jnp.maximum(jnp.maximum(ms[0], ms[1]), ms[2])
        ws = [jnp.exp(m - mm) for m in ms]
        num = sum(w * acc_scr[g, rows, :] for g, w in enumerate(ws))
        den = sum(w * l_scr[g, rows, :] for g, w in enumerate(ws))
        o_ref[rows, :] = (num / den).astype(o_ref.dtype)
        return carry

    lax.fori_loop(0, nblk, merge, 0)


def _dil_attn(qkv0, qkv12, bias, batch, seq):
    npair = A_HEADS // 2
    nh = npair
    q3 = qkv0.reshape(batch, seq, -1)
    f3 = qkv12.reshape(batch, seq, -1)

    def spec(off):
        return pl.BlockSpec((None, seq, LANES), lambda b, p, off=off: (b, 0, off + p))

    in_specs = ([spec(0), spec(nh), spec(2 * nh)]
                + [spec(0), spec(nh), spec(2 * nh)]
                + [spec(3 * nh), spec(4 * nh), spec(5 * nh)]
                + [pl.BlockSpec((N_DIL, 2, BLOCK, 2 * BLOCK), lambda b, p: (0, p, 0, 0))])
    out = pl.pallas_call(
        functools.partial(_dil_attn_kernel, seq),
        grid=(batch, npair),
        in_specs=in_specs,
        out_specs=pl.BlockSpec((None, seq, LANES), lambda b, p: (b, 0, p)),
        out_shape=jax.ShapeDtypeStruct((batch, seq, A_HEADS * HEAD_DIM), BF16),
        scratch_shapes=[pltpu.VMEM((N_DIL, seq, LANES), F32)] * 3,
        compiler_params=_params(("parallel", "parallel")),
    )(q3, q3, q3, f3, f3, f3, f3, f3, f3, bias)
    return out.reshape(batch * seq, A_HEADS * HEAD_DIM)


def _swa_attn_kernel(seq, sinks_ref, q_ref, k_ref, v_ref, bias_ref, o_ref):
    nblk = seq // BLOCK
    p = pl.program_id(1)
    sinks = (sinks_ref[p], sinks_ref[p + SWA_Q_HEADS // 2])

    def unit(qrows, krows, with_prev):
        q_lo, q_hi = _split_pair(q_ref[qrows, :].astype(F32))
        lo = 0 if with_prev else BLOCK
        acc, _, l = _band_unit(q_lo, q_hi, [(k_ref[krows, :], v_ref[krows, :],
                                             bias_ref[0, :, lo:], bias_ref[1, :, lo:])], sinks)
        o_ref[qrows, :] = (acc / l).astype(o_ref.dtype)

    unit(pl.ds(0, BLOCK), pl.ds(0, BLOCK), False)

    def body(j, carry):
        start = pl.multiple_of(j * BLOCK, BLOCK)
        prev = pl.multiple_of(j * BLOCK - BLOCK, BLOCK)
        unit(pl.ds(start, BLOCK), pl.ds(prev, 2 * BLOCK), True)
        return carry

    lax.fori_loop(1, nblk, body, 0)


def _swa_attn(qkv, bias, sinks, batch, seq):
    npair = SWA_Q_HEADS // 2
    q3 = qkv.reshape(batch, seq, -1)
    out = pl.pallas_call(
        functools.partial(_swa_attn_kernel, seq),
        grid=(batch, npair),
        in_specs=[
            pl.BlockSpec(memory_space=pltpu.SMEM),
            pl.BlockSpec((None, seq, LANES), lambda b, p: (b, 0, p)),
            pl.BlockSpec((None, seq, LANES), lambda b, p: (b, 0, npair)),
            pl.BlockSpec((None, seq, LANES), lambda b, p: (b, 0, npair + 1)),
            pl.BlockSpec((None, 2, BLOCK, 2 * BLOCK), lambda b, p: (p, 0, 0, 0)),
        ],
        out_specs=pl.BlockSpec((None, seq, LANES), lambda b, p: (b, 0, p)),
        out_shape=jax.ShapeDtypeStruct((batch, seq, SWA_Q_HEADS * HEAD_DIM), BF16),
        compiler_params=_params(("parallel", "parallel")),
    )(sinks, q3, q3, q3, bias)
    return out.reshape(batch * seq, SWA_Q_HEADS * HEAD_DIM)


def _causal_two_maps(seq, q_of, k_of, v_ref, bias_of, s_scr, p_scr, finish):
    nblk = seq // BLOCK
    for i in range(nblk):
        rows = slice(i * BLOCK, (i + 1) * BLOCK)
        nchunk = i * BLOCK // KV_CHUNK + 1
        width = nchunk * KV_CHUNK
        res = []
        for mp in range(2):
            qm = q_of(rows, mp)
            mx = None
            for c in range(nchunk):
                cols = slice(c * KV_CHUNK, (c + 1) * KV_CHUNK)
                s = _dot_nt(qm, k_of(cols, mp))
                b = bias_of(i - c * (KV_CHUNK // BLOCK), mp, c == nchunk - 1)
                if b is not None:
                    s = s + b
                s_scr[mp, :, cols] = s
                t = jnp.maximum(s[:, :LANES], s[:, LANES:])
                mx = t if mx is None else jnp.maximum(mx, t)
            m = jnp.max(mx, axis=-1, keepdims=True)
            ls = None
            for c in range(nchunk):
                cols = slice(c * KV_CHUNK, (c + 1) * KV_CHUNK)
                p = jnp.exp(s_scr[mp, :, cols] - m)
                t = p[:, :LANES] + p[:, LANES:]
                ls = t if ls is None else ls + t
                p_scr[mp, :, cols] = p.astype(BF16)
            l = jnp.sum(ls, axis=-1, keepdims=True)
            acc = _dot(p_scr[mp, :, :width], v_ref[:width, :])
            res.append((acc, l))
        finish(rows, res)


def _diff_attn_kernel(seq, lam_init, q_ref, k_ref, v_ref, bias_ref, lq1, lk1, lq2, lk2, subln_ref,
                      o_ref, s_scr, p_scr):
    lo = (lax.broadcasted_iota(jnp.int32, (BLOCK, LANES), 1) < HEAD_DIM).astype(F32).astype(BF16)
    hi = (lax.broadcasted_iota(jnp.int32, (BLOCK, LANES), 1) >= HEAD_DIM).astype(F32).astype(BF16)
    lam = (jnp.exp(jnp.sum(lq1[...] * lk1[...], axis=-1, keepdims=True))
           - jnp.exp(jnp.sum(lq2[...] * lk2[...], axis=-1, keepdims=True)) + lam_init)

    def q_of(rows, mp):
        return q_ref[rows, :] * (lo, hi)[mp]

    def k_of(cols, mp):
        return k_ref[cols, :]

    def bias_of(delta, mp, last):
        return bias_ref[delta, mp]

    def finish(rows, res):
        (a0, l0), (a1, l1) = res
        o = a0 * (1.0 / l0) - a1 * (lam / l1)
        o = _rms(o, subln_ref[...]) * (1.0 - lam_init)
        o_ref[rows, :] = o.astype(o_ref.dtype)

    _causal_two_maps(seq, q_of, k_of, v_ref, bias_of, s_scr, p_scr, finish)


def _diff_attn(qkv, bias, lams, subln, lam_init, batch, seq):
    nh = DIFF_HEADS
    q3 = qkv.reshape(batch, seq, -1)
    ndelta = seq // BLOCK
    vec = pl.BlockSpec((1, HEAD_DIM), lambda h, b: (0, 0))
    out = pl.pallas_call(
        functools.partial(_diff_attn_kernel, seq, lam_init),
        grid=(nh, batch),
        in_specs=[
            pl.BlockSpec((None, seq, LANES), lambda h, b: (b, 0, h)),
            pl.BlockSpec((None, seq, LANES), lambda h, b: (b, 0, nh + h)),
            pl.BlockSpec((None, seq, LANES), lambda h, b: (b, 0, 2 * nh + h)),
            pl.BlockSpec((None, ndelta, 2, BLOCK, KV_CHUNK), lambda h, b: (h, 0, 0, 0, 0)),
            vec, vec, vec, vec,
            pl.BlockSpec((1, LANES), lambda h, b: (0, 0)),
        ],
        out_specs=pl.BlockSpec((None, seq, LANES), lambda h, b: (b, 0, h)),
        out_shape=jax.ShapeDtypeStruct((batch, seq, nh * LANES), BF16),
        scratch_shapes=[pltpu.VMEM((2, BLOCK, seq), F32), pltpu.VMEM((2, BLOCK, seq), BF16)],
        compiler_params=_params(("parallel", "parallel")),
    )(q3, q3, q3, bias, *lams, subln)
    return out.reshape(batch * seq, nh * LANES)


def _mla_attn_kernel(seq, q_ref, k_ref, v_ref, mask_ref, o_ref, s_scr, p_scr):
    def q_of(rows, mp):
        return q_ref[rows, mp * LANES:(mp + 1) * LANES]

    def k_of(cols, mp):
        return k_ref[cols, mp * LANES:(mp + 1) * LANES]

    def bias_of(delta, mp, last):
        return mask_ref[delta] if last else None

    def finish(rows, res):
        (a0, l0), (a1, l1) = res
        o = jnp.where(_lane_lo((BLOCK, LANES)), a0 * (1.0 / l0), a1 * (1.0 / l1))
        o_ref[rows, :] = o.astype(o_ref.dtype)

    _causal_two_maps(seq, q_of, k_of, v_ref, bias_of, s_scr, p_scr, finish)


def _mla_attn(q, k, v, mask, batch, seq):
    npair = MLA_HEADS // 2
    out = pl.pallas_call(
        functools.partial(_mla_attn_kernel, seq),
        grid=(npair, batch),
        in_specs=[
            pl.BlockSpec((None, seq, 2 * LANES), lambda p, b: (b, 0, p)),
            pl.BlockSpec((None, seq, 2 * LANES), lambda p, b: (b, 0, p)),
            pl.BlockSpec((None, seq, LANES), lambda p, b: (b, 0, p)),
            pl.BlockSpec((2, BLOCK, KV_CHUNK), lambda p, b: (0, 0, 0)),
        ],
        out_specs=pl.BlockSpec((None, seq, LANES), lambda p, b: (b, 0, p)),
        out_shape=jax.ShapeDtypeStruct((batch, seq, MLA_HEADS * MLA_V), BF16),
        scratch_shapes=[pltpu.VMEM((2, BLOCK, seq), F32), pltpu.VMEM((2, BLOCK, seq), BF16)],
        compiler_params=_params(("parallel", "parallel")),
    )(q.reshape(batch, seq, -1), k.reshape(batch, seq, -1), v.reshape(batch, seq, -1), mask)
    return out.reshape(batch * seq, MLA_HEADS * MLA_V)


def _mla_proj_kernel(scale, x_ref, g_ref, w_in_ref, qa_ref, kva_ref, wq_ref, wk_ref, wv_ref,
                     gq_ref, gk_ref, cos_ref, sin_ref, q_out, k_out, v_out):
    hn = _rms(x_ref[...], g_ref[...]).astype(BF16)
    lat = _dot(hn, w_in_ref[...])
    cq = _rms(lat[:, :MLA_Q_RANK], qa_ref[...]).astype(BF16)
    ckv = _rms(lat[:, MLA_Q_RANK:MLA_Q_RANK + MLA_KV_RANK], kva_ref[...]).astype(BF16)
    kpe = lat[:, MLA_Q_RANK + MLA_KV_RANK:]
    v_out[...] = _dot(ckv, wv_ref[...]).astype(v_out.dtype)
    real = lax.broadcasted_iota(jnp.int32, (x_ref.shape[0], LANES), 1) < MLA_QK
    cos = cos_ref[...]
    sin = sin_ref[...]

    def head(y, gain):
        ms = jnp.sum(jnp.where(real, y * y, 0.0), axis=-1, keepdims=True) * (1.0 / MLA_QK)
        yn = y * lax.rsqrt(ms + RMS_EPS) * gain
        return yn * cos + pltpu.roll(yn, LANES - MLA_ROPE, 1) * sin

    for hp in range(MLA_HEADS // 2):
        cols = slice(hp * 2 * LANES, (hp + 1) * 2 * LANES)
        yq = _dot(cq, wq_ref[:, cols])
        yk = _dot(ckv, wk_ref[:, cols])
        for j in range(2):
            lanes = slice(j * LANES, (j + 1) * LANES)
            out_cols = slice((2 * hp + j) * LANES, (2 * hp + j + 1) * LANES)
            q_out[:, out_cols] = (head(yq[:, lanes], gq_ref[...]) * scale).astype(q_out.dtype)
            k_out[:, out_cols] = head(yk[:, lanes] + kpe, gk_ref[...]).astype(k_out.dtype)


def _mla_proj(x2d, g, w_in, qa, kva, wq, wk, wv, gq, gk, cos_t, sin_t, seq):
    t = x2d.shape[0]
    per_seq = seq // ROW_TILE
    nq = MLA_HEADS * LANES
    tab = pl.BlockSpec((ROW_TILE, LANES), lambda i: (i % per_seq, 0))
    return pl.pallas_call(
        functools.partial(_mla_proj_kernel, MLA_QK ** -0.5),
        grid=(t // ROW_TILE,),
        in_specs=[
            pl.BlockSpec((ROW_TILE, D_MODEL), lambda i: (i, 0)),
            _const_spec((1, D_MODEL)),
            _const_spec(w_in.shape),
            _const_spec((1, MLA_Q_RANK)),
            _const_spec((1, MLA_KV_RANK)),
            _const_spec(wq.shape),
            _const_spec(wk.shape),
            _const_spec(wv.shape),
            _const_spec((1, LANES)),
            _const_spec((1, LANES)),
            tab, tab,
        ],
        out_specs=[pl.BlockSpec((ROW_TILE, nq), lambda i: (i, 0)),
                   pl.BlockSpec((ROW_TILE, nq), lambda i: (i, 0)),
                   pl.BlockSpec((ROW_TILE, MLA_HEADS * MLA_V), lambda i: (i, 0))],
        out_shape=[jax.ShapeDtypeStruct((t, nq), BF16), jax.ShapeDtypeStruct((t, nq), BF16),
                   jax.ShapeDtypeStruct((t, MLA_HEADS * MLA_V), BF16)],
        compiler_params=_params(("parallel",)),
    )(x2d, g, w_in, qa, kva, wq, wk, wv, gq, gk, cos_t, sin_t)


def _ffn_kernel(tiles_per_seq, x_ref, o_ref, wo_ref, g_ref, wup_ref, cw_ref, cb_ref, wdn_ref,
                out_ref, h_scr, act_scr, conv_scr, tail_scr):
    rows = x_ref.shape[0]
    x2 = x_ref[...] + _dot(o_ref[...], wo_ref[...])
    out_ref[...] = x2
    h_scr[...] = _rms(x2, g_ref[...]).astype(BF16)
    first = pl.program_id(0) % tiles_per_seq == 0
    for c in range(D_FF // FF_CHUNK):
        cols = slice(c * FF_CHUNK, (c + 1) * FF_CHUNK)
        gate = _dot(h_scr[...], wup_ref[:, cols])
        up = _dot(h_scr[...], wup_ref[:, D_FF + c * FF_CHUNK:D_FF + (c + 1) * FF_CHUNK])
        conv_scr[0:8, :] = jnp.where(first, 0.0, tail_scr[:, cols])
        conv_scr[8:, :] = gate
        tail_scr[:, cols] = gate[rows - 8:, :]
        conv = (cb_ref[:, cols] + cw_ref[2:3, cols] * gate
                + cw_ref[1:2, cols] * conv_scr[7:7 + rows, :]
                + cw_ref[0:1, cols] * conv_scr[6:6 + rows, :])
        act_scr[:, cols] = (conv / (1.0 + jnp.exp(-conv)) * up).astype(BF16)
    out_ref[...] += _dot(act_scr[...], wdn_ref[...])


def _ffn(x2d, o, wo, g, wup, cw, cb, wdn, seq):
    t = x2d.shape[0]
    ko = o.shape[1]
    return pl.pallas_call(
        functools.partial(_ffn_kernel, seq // ROW_TILE),
        grid=(t // ROW_TILE,),
        in_specs=[
            pl.BlockSpec((ROW_TILE, D_MODEL), lambda i: (i, 0)),
            pl.BlockSpec((ROW_TILE, ko), lambda i: (i, 0)),
            _const_spec((ko, D_MODEL)),
            _const_spec((1, D_MODEL)),
            _const_spec((D_MODEL, 2 * D_FF)),
            _const_spec((CONV_WIDTH, D_FF)),
            _const_spec((1, D_FF)),
            _const_spec((D_FF, D_MODEL)),
        ],
        out_specs=pl.BlockSpec((ROW_TILE, D_MODEL), lambda i: (i, 0)),
        out_shape=jax.ShapeDtypeStruct((t, D_MODEL), F32),
        scratch_shapes=[
            pltpu.VMEM((ROW_TILE, D_MODEL), BF16),
            pltpu.VMEM((ROW_TILE, D_FF), BF16),
            pltpu.VMEM((ROW_TILE + 8, FF_CHUNK), F32),
            pltpu.VMEM((8, D_FF), F32),
        ],
        compiler_params=_params(("arbitrary",)),
    )(x2d, o, wo, g, wup, cw, cb, wdn)


def _t5_bucket(dist):
    max_exact = NUM_BUCKETS // 2
    d_f = jnp.maximum(dist, 1).astype(F32)
    large = max_exact + (jnp.log(d_f / max_exact) / math.log(MAX_DISTANCE / max_exact)
                         * (NUM_BUCKETS - max_exact)).astype(jnp.int32)
    return jnp.where(dist < max_exact, dist, jnp.minimum(large, NUM_BUCKETS - 1))


def _band_bias(table_cols, dilation, window):
    offset = jnp.arange(BLOCK)[:, None] + BLOCK - jnp.arange(2 * BLOCK)[None, :]
    bias = table_cols.T[:, _t5_bucket(jnp.maximum(offset, 0) * dilation)].astype(F32)
    return jnp.where((offset >= 0) & (offset <= window), bias, NEG)


def _causal_bias(tab, seq):
    delta = jnp.arange(seq // BLOCK)[:, None, None] * BLOCK
    dist = delta + jnp.arange(BLOCK)[None, :, None] - jnp.arange(KV_CHUNK)[None, None, :]
    bias = tab[:, :, _t5_bucket(jnp.maximum(dist, 0))]
    bias = jnp.where(dist >= 0, bias, NEG)
    return bias.transpose(0, 2, 1, 3, 4).astype(F32)


def _causal_mask(ndelta):
    delta = jnp.arange(ndelta)[:, None, None] * BLOCK
    dist = delta + jnp.arange(BLOCK)[None, :, None] - jnp.arange(KV_CHUNK)[None, None, :]
    return jnp.where(dist >= 0, 0.0, NEG).astype(F32)


def _seg_ones():
    i = jnp.arange(MXU_DIM)
    return ((i[:, None] // HEAD_DIM == i[None, :] // HEAD_DIM).astype(F32) / HEAD_DIM).astype(BF16)


def kernel(x, rel_bias_table, norm_mix, norm_ffn, a_w_in, a_q_norm, a_k_norm, a_w_out, b_w_in, b_q_a_norm, b_kv_a_norm, b_w_q_up, b_w_kv_up, b_q_norm, b_k_norm, b_w_out, c_w_in, c_q_norm, c_k_norm, c_lambda_q1, c_lambda_k1, c_lambda_q2, c_lambda_k2, c_subln, c_w_out, d_w_in, d_q_norm, d_k_norm, d_sinks, d_w_out, f_w_up, f_conv_w, f_conv_b, f_w_down):
    batch, seq, _ = x.shape
    depth = norm_mix.shape[0]
    assert seq == MAX_DISTANCE and seq % ROW_TILE == 0
    table = rel_bias_table.astype(F32)
    bd = _seg_ones()
    qk_scale = HEAD_DIM ** -0.5
    row = lambda v: v.reshape(1, -1).astype(F32)
    xf = x.reshape(batch * seq, D_MODEL)

    for i in range(depth):
        m, j = i % 4, i // 4
        g_mix = row(norm_mix[i])
        if m == 0:
            hw = A_HEADS * HEAD_DIM
            gain = jnp.concatenate(
                [jnp.concatenate([jnp.tile(a_q_norm[j, g], A_HEADS) * qk_scale,
                                  jnp.tile(a_k_norm[j, g], A_HEADS),
                                  jnp.ones((hw,), F32)]) for g in range(N_DIL)])
            segs = []
            for g in range(N_DIL):
                for t in range(3):
                    for half in range(hw // MXU_DIM):
                        c0 = (g * 3 + t) * hw + half * MXU_DIM
                        oc = c0 if g == 0 else c0 - 3 * hw
                        segs.append((c0, MXU_DIM, t < 2, 0 if g == 0 else 1, oc))
            qkv0, qkv12 = _proj(xf, g_mix, a_w_in[j].astype(BF16), row(gain), bd, tuple(segs),
                                ((3 * hw, BF16), (6 * hw, F32)))
            bias = jnp.stack([_band_bias(table[:, :A_HEADS], dil, window // dil)
                              for window, dil in DIL_GROUPS])
            o = _dil_attn(qkv0, qkv12, bias, batch, seq)
            w_out = a_w_out[j]
        elif m == 1:
            w_in = b_w_in[j]
            lat_w = MLA_Q_RANK + MLA_KV_RANK
            pe = w_in[:, lat_w:]
            half = MLA_ROPE // 2
            swap = lambda a: jnp.concatenate([a[..., half:], a[..., :half]], axis=-1)
            w_in_ext = jnp.concatenate(
                [w_in[:, :lat_w], jnp.zeros((D_MODEL, MLA_NOPE), F32), pe, swap(pe)], axis=1)
            wq = b_w_q_up[j].reshape(MLA_Q_RANK, MLA_HEADS, MLA_QK)
            wq = jnp.concatenate([wq, swap(wq[..., MLA_NOPE:])], axis=-1).reshape(MLA_Q_RANK, -1)
            wkv = b_w_kv_up[j].reshape(MLA_KV_RANK, MLA_HEADS, MLA_NOPE + MLA_V)
            wk = jnp.concatenate([wkv[..., :MLA_NOPE], jnp.zeros_like(wkv[..., :MLA_NOPE])],
                                 axis=-1).reshape(MLA_KV_RANK, -1)
            wv = wkv[..., MLA_NOPE:].reshape(MLA_KV_RANK, -1)
            pad_gain = lambda gn: row(jnp.concatenate([gn, swap(gn[MLA_NOPE:])]))
            inv_freq = ROPE_THETA ** (-jnp.arange(0, MLA_ROPE, 2, dtype=F32) / MLA_ROPE)
            ang = jnp.arange(seq, dtype=F32)[:, None] * inv_freq[None, :]
            cos, sin = jnp.cos(ang), jnp.sin(ang)
            ones = jnp.ones((seq, MLA_NOPE), F32)
            zeros = jnp.zeros((seq, MLA_ROPE), F32)
            cos_t = jnp.concatenate([ones, cos, cos, zeros], axis=1)
            sin_t = jnp.concatenate([0.0 * ones, -sin, sin, zeros], axis=1)
            q, k, v = _mla_proj(xf, g_mix, w_in_ext.astype(BF16), row(b_q_a_norm[j]),
                                row(b_kv_a_norm[j]), wq.astype(BF16), wk.astype(BF16),
                                wv.astype(BF16), pad_gain(b_q_norm[j]), pad_gain(b_k_norm[j]),
                                cos_t, sin_t, seq)
            o = _mla_attn(q, k, v, _causal_mask(KV_CHUNK // BLOCK), batch, seq)
            w_out = b_w_out[j]
        elif m == 2:
            qk_w = DIFF_HEADS * 2 * HEAD_DIM
            gain = jnp.concatenate([jnp.tile(c_q_norm[j], 2 * DIFF_HEADS) * qk_scale,
                                    jnp.tile(c_k_norm[j], 2 * DIFF_HEADS), jnp.ones((qk_w,), F32)])
            segs = tuple((c0, MXU_DIM, c0 < 2 * qk_w, 0, c0) for c0 in range(0, 3 * qk_w, MXU_DIM))
            (qkv,) = _proj(xf, g_mix, c_w_in[j].astype(BF16), row(gain), bd, segs,
                           ((3 * qk_w, BF16),))
            tab = table[:, :2 * DIFF_HEADS].T.reshape(2, DIFF_HEADS, NUM_BUCKETS).transpose(1, 0, 2)
            lam_init = 0.8 - 0.6 * math.exp(-0.3 * i)
            lams = [row(v) for v in (c_lambda_q1[j], c_lambda_k1[j], c_lambda_q2[j], c_lambda_k2[j])]
            o = _diff_attn(qkv, _causal_bias(tab, seq), lams, row(c_subln[j]), lam_init, batch, seq)
            w_out = c_w_out[j]
        else:
            npair = SWA_Q_HEADS // 2
            qw = SWA_Q_HEADS * HEAD_DIM
            order = jnp.arange(SWA_Q_HEADS).reshape(2, npair).T.reshape(-1)
            w_in = d_w_in[j]
            wq = w_in[:, :qw].reshape(D_MODEL, SWA_Q_HEADS, HEAD_DIM)[:, order].reshape(D_MODEL, qw)
            w_perm = jnp.concatenate([wq, w_in[:, qw:]], axis=1)
            kvw = SWA_KV_HEADS * HEAD_DIM
            gain = jnp.concatenate([jnp.tile(d_q_norm[j], SWA_Q_HEADS) * qk_scale,
                                    jnp.tile(d_k_norm[j], SWA_KV_HEADS), jnp.ones((kvw,), F32)])
            segs = tuple((c0, MXU_DIM, True, 0, c0) for c0 in range(0, qw, MXU_DIM))
            segs += ((qw, kvw, True, 0, qw), (qw + kvw, kvw, False, 0, qw + kvw))
            (qkv,) = _proj(xf, g_mix, w_perm.astype(BF16), row(gain), bd, segs,
                           ((qw + 2 * kvw, BF16),))
            bias = _band_bias(table[:, :SWA_Q_HEADS], 1, SWA_WINDOW - 1)
            bias = bias[order].reshape(npair, 2, BLOCK, 2 * BLOCK)
            o = _swa_attn(qkv, bias, d_sinks[j].astype(F32), batch, seq)
            w_out = d_w_out[j].reshape(SWA_Q_HEADS, HEAD_DIM, D_MODEL)[order].reshape(qw, D_MODEL)
        xf = _ffn(xf, o, w_out.astype(BF16), row(norm_ffn[i]), f_w_up[i].astype(BF16),
                  f_conv_w[i].astype(F32), row(f_conv_b[i]), f_w_down[i].astype(BF16), seq)
    return xf.reshape(batch, seq, D_MODEL)
```

```python
import functools
import math

import jax
import jax.numpy as jnp
from jax import lax
from jax.experimental import pallas as pl
from jax.experimental.pallas import tpu as pltpu

F32 = jnp.float32
BF16 = jnp.bfloat16

D_MODEL = 1024
HEAD_DIM = 64
BLOCK = 128
RMS_EPS = 1e-6
NUM_BUCKETS = 32
MAX_DISTANCE = 2048
DIL_GROUPS = ((128, 1), (512, 4), (2048, 16))
N_DIL = 3
A_HEADS = 8
MLA_HEADS = 16
MLA_NOPE = 64
MLA_ROPE = 32
MLA_QK = 96
MLA_V = 64
MLA_Q_RANK = 384
MLA_KV_RANK = 256
ROPE_THETA = 10000.0
DIFF_HEADS = 8
SWA_Q_HEADS = 16
SWA_KV_HEADS = 2
SWA_WINDOW = 128
D_FF = 2816
CONV_WIDTH = 3

LANES = 128
MXU_DIM = 256
NEG = -1e30
LOG2E = math.log2(math.e)
ROW_TILE = 512
FF_CHUNK = 256
KV_CHUNK = 256
VMEM_LIMIT = 56 * 1024 * 1024


def _dot(a, b):
    return jnp.dot(a, b, preferred_element_type=F32)


def _dot_nt(a, b):
    return lax.dot_general(a, b, (((1,), (1,)), ((), ())), preferred_element_type=F32)


def _rms(x, g):
    ms = jnp.mean(x * x, axis=-1, keepdims=True)
    return x * lax.rsqrt(ms + RMS_EPS) * g


def _const_spec(shape):
    nd = len(shape)
    return pl.BlockSpec(shape, lambda *_: (0,) * nd, pipeline_mode=pl.Buffered(1))


def _params(sem):
    return pltpu.CompilerParams(dimension_semantics=sem, vmem_limit_bytes=VMEM_LIMIT)


def _lane_lo(shape):
    return lax.broadcasted_iota(jnp.int32, shape, len(shape) - 1) < HEAD_DIM


def _seg_norm(y, bd_ref):
    w = y.shape[-1]
    bd = bd_ref[:w, :w]
    y2 = y * y
    hi = y2.astype(BF16)
    lo = (y2 - hi.astype(F32)).astype(BF16)
    ms = _dot(hi, bd) + _dot(lo, bd)
    return y * lax.rsqrt(ms + RMS_EPS)


def _proj_kernel(segs, x_ref, g_ref, w_ref, gain_ref, bd_ref, *out_refs):
    hn = _rms(x_ref[...], g_ref[...]).astype(BF16)
    for (c0, width, norm, oi, oc) in segs:
        y = _dot(hn, w_ref[:, c0:c0 + width])
        if norm:
            y = _seg_norm(y, bd_ref) * gain_ref[:, c0:c0 + width]
        out_refs[oi][:, oc:oc + width] = y.astype(out_refs[oi].dtype)


def _proj(x2d, g, w, gain, bd, segs, outs):
    t = x2d.shape[0]
    n = w.shape[1]
    return pl.pallas_call(
        functools.partial(_proj_kernel, segs),
        grid=(t // ROW_TILE,),
        in_specs=[
            pl.BlockSpec((ROW_TILE, D_MODEL), lambda i: (i, 0)),
            _const_spec((1, D_MODEL)),
            _const_spec((D_MODEL, n)),
            _const_spec((1, n)),
            _const_spec((MXU_DIM, MXU_DIM)),
        ],
        out_specs=[pl.BlockSpec((ROW_TILE, ow), lambda i: (i, 0)) for ow, _ in outs],
        out_shape=[jax.ShapeDtypeStruct((t, ow), od) for ow, od in outs],
        compiler_params=_params(("parallel",)),
    )(x2d, g, w, gain, bd)


def _band_unit(q_lo, q_hi, segs, sinks=None):
    res = []
    for h, qh in enumerate((q_lo, q_hi)):
        ss = [_dot_nt(qh, k) + (b_lo, b_hi)[h] for (k, _, b_lo, b_hi) in segs]
        m = jnp.max(functools.reduce(jnp.maximum, ss), axis=-1, keepdims=True)
        if sinks is not None:
            m = jnp.maximum(m, sinks[h])
        ps = [jnp.exp2(s - m) for s in ss]
        l = jnp.sum(functools.reduce(jnp.add, ps), axis=-1, keepdims=True)
        if sinks is not None:
            l = l + jnp.exp2(sinks[h] - m)
        acc = sum(_dot(p.astype(BF16), v) for p, (_, v, _, _) in zip(ps, segs))
        res.append((acc, m, l))
    lo = _lane_lo((BLOCK, LANES))
    (a0, m0, l0), (a1, m1, l1) = res
    return (jnp.where(lo, a0, a1),
            jnp.where(lo, jnp.broadcast_to(m0, (BLOCK, LANES)), jnp.broadcast_to(m1, (BLOCK, LANES))),
            jnp.where(lo, jnp.broadcast_to(l0, (BLOCK, LANES)), jnp.broadcast_to(l1, (BLOCK, LANES))))


def _split_pair(q):
    lo = _lane_lo(q.shape)
    zero = jnp.zeros_like(q)
    return jnp.where(lo, q, zero).astype(BF16), jnp.where(lo, zero, q).astype(BF16)


def _dil_attn_kernel(seq, q0, k0, v0, q1, k1, v1, q2, k2, v2, bias_ref, o_ref,
                     acc_scr, m_scr, l_scr):
    nblk = seq // BLOCK
    refs = ((q0, k0, v0), (q1, k1, v1), (q2, k2, v2))

    def bias(g, h, lo, hi):
        return bias_ref[h, g, :, lo:hi]

    for g, (_, dil) in enumerate(DIL_GROUPS):
        qr, kr, vr = refs[g]
        for r in range(dil):
            for j in range(nblk // dil):
                def rows_of(jj):
                    start = r + dil * BLOCK * jj
                    return pl.ds(start, BLOCK) if dil == 1 else pl.ds(start, BLOCK, stride=dil)

                rows = rows_of(j)
                q_lo, q_hi = _split_pair(qr[rows, :].astype(F32))
                if j > 0 and dil == 1:
                    both = pl.ds((j - 1) * BLOCK, 2 * BLOCK)
                    segs = [(kr[both, :], vr[both, :],
                             bias(g, 0, 0, 2 * BLOCK), bias(g, 1, 0, 2 * BLOCK))]
                else:
                    segs = [(kr[rows, :].astype(BF16), vr[rows, :].astype(BF16),
                             bias(g, 0, BLOCK, 2 * BLOCK), bias(g, 1, BLOCK, 2 * BLOCK))]
                    if j > 0:
                        prow = rows_of(j - 1)
                        segs.append((kr[prow, :].astype(BF16), vr[prow, :].astype(BF16),
                                     bias(g, 0, 0, BLOCK), bias(g, 1, 0, BLOCK)))
                acc, m, l = _band_unit(q_lo, q_hi, segs)
                acc_scr[g, rows, :] = acc
                m_scr[g, rows, :] = m
                l_scr[g, rows, :] = l

    for c in range(nblk):
        rows = pl.ds(c * BLOCK, BLOCK)
        ms = [m_scr[g, rows, :] for g in range(N_DIL)]
        mm = jnp.maximum(jnp.maximum(ms[0], ms[1]), ms[2])
        ws = [jnp.exp2(m - mm) for m in ms]
        num = sum(w * acc_scr[g, rows, :] for g, w in enumerate(ws))
        den = sum(w * l_scr[g, rows, :] for g, w in enumerate(ws))
        o_ref[rows, :] = (num / den).astype(o_ref.dtype)


def _dil_attn(qkv0, qkv12, bias, batch, seq):
    npair = A_HEADS // 2
    nh = npair
    q3 = qkv0.reshape(batch, seq, -1)
    f3 = qkv12.reshape(batch, seq, -1)

    def spec(off):
        return pl.BlockSpec((None, seq, LANES), lambda b, p, off=off: (b, 0, off + p))

    in_specs = ([spec(0), spec(nh), spec(2 * nh)]
                + [spec(0), spec(nh), spec(2 * nh)]
                + [spec(3 * nh), spec(4 * nh), spec(5 * nh)]
                + [pl.BlockSpec((2, N_DIL, BLOCK, 2 * BLOCK), lambda b, p: (p, 0, 0, 0))])
    out = pl.pallas_call(
        functools.partial(_dil_attn_kernel, seq),
        grid=(batch, npair),
        in_specs=in_specs,
        out_specs=pl.BlockSpec((None, seq, LANES), lambda b, p: (b, 0, p)),
        out_shape=jax.ShapeDtypeStruct((batch, seq, A_HEADS * HEAD_DIM), BF16),
        scratch_shapes=[pltpu.VMEM((N_DIL, seq, LANES), F32)] * 3,
        compiler_params=_params(("parallel", "parallel")),
    )(q3, q3, q3, f3, f3, f3, f3, f3, f3, bias)
    return out.reshape(batch * seq, A_HEADS * HEAD_DIM)


def _swa_attn_kernel(seq, sinks_ref, q_ref, k_ref, v_ref, blo_ref, bhi_ref, o_ref):
    p = pl.program_id(1)
    sinks = (sinks_ref[p] * LOG2E, sinks_ref[p + SWA_Q_HEADS // 2] * LOG2E)
    for j in range(seq // BLOCK):
        qrows = pl.ds(j * BLOCK, BLOCK)
        krows = pl.ds((j - 1) * BLOCK, 2 * BLOCK) if j > 0 else qrows
        lo = 0 if j > 0 else BLOCK
        q_lo, q_hi = _split_pair(q_ref[qrows, :].astype(F32))
        acc, _, l = _band_unit(q_lo, q_hi, [(k_ref[krows, :], v_ref[krows, :],
                                             blo_ref[0, :, lo:], bhi_ref[0, :, lo:])], sinks)
        o_ref[qrows, :] = (acc / l).astype(o_ref.dtype)


def _swa_attn(qkv, bias, sinks, batch, seq):
    npair = SWA_Q_HEADS // 2
    q3 = qkv.reshape(batch, seq, -1)
    out = pl.pallas_call(
        functools.partial(_swa_attn_kernel, seq),
        grid=(batch, npair),
        in_specs=[
            pl.BlockSpec(memory_space=pltpu.SMEM),
            pl.BlockSpec((None, seq, LANES), lambda b, p: (b, 0, p)),
            pl.BlockSpec((None, seq, LANES), lambda b, p: (b, 0, npair)),
            pl.BlockSpec((None, seq, LANES), lambda b, p: (b, 0, npair + 1)),
            pl.BlockSpec((None, 1, BLOCK, 2 * BLOCK), lambda b, p: (p, 0, 0, 0)),
            pl.BlockSpec((None, 1, BLOCK, 2 * BLOCK), lambda b, p: (p + npair, 0, 0, 0)),
        ],
        out_specs=pl.BlockSpec((None, seq, LANES), lambda b, p: (b, 0, p)),
        out_shape=jax.ShapeDtypeStruct((batch, seq, SWA_Q_HEADS * HEAD_DIM), BF16),
        compiler_params=_params(("parallel", "parallel")),
    )(sinks, q3, q3, q3, bias, bias)
    return out.reshape(batch * seq, SWA_Q_HEADS * HEAD_DIM)


def _causal_two_maps(seq, q_of, k_of, v_ref, bias_of, s_scr, p_scr, finish):
    nblk = seq // BLOCK
    for i in range(nblk):
        rows = slice(i * BLOCK, (i + 1) * BLOCK)
        nchunk = i * BLOCK // KV_CHUNK + 1
        width = nchunk * KV_CHUNK
        res = []
        for mp in range(2):
            qm = q_of(rows, mp)
            mx = None
            for c in range(nchunk):
                cols = slice(c * KV_CHUNK, (c + 1) * KV_CHUNK)
                s = _dot_nt(qm, k_of(cols, mp))
                b = bias_of(i - c * (KV_CHUNK // BLOCK), mp, c == nchunk - 1)
                if b is not None:
                    s = s + b
                s_scr[mp, :, cols] = s
                t = jnp.maximum(s[:, :LANES], s[:, LANES:])
                mx = t if mx is None else jnp.maximum(mx, t)
            m = jnp.max(mx, axis=-1, keepdims=True)
            ls = None
            for c in range(nchunk):
                cols = slice(c * KV_CHUNK, (c + 1) * KV_CHUNK)
                p = jnp.exp2(s_scr[mp, :, cols] - m)
                t = p[:, :LANES] + p[:, LANES:]
                ls = t if ls is None else ls + t
                p_scr[mp, :, cols] = p.astype(BF16)
            l = jnp.sum(ls, axis=-1, keepdims=True)
            acc = _dot(p_scr[mp, :, :width], v_ref[:width, :])
            res.append((acc, l))
        finish(rows, res)


def _diff_attn_kernel(seq, lam_init, q_ref, k_ref, v_ref, bias0_ref, bias1_ref, lq1, lk1, lq2, lk2,
                      subln_ref, o_ref, s_scr, p_scr):
    lo = (lax.broadcasted_iota(jnp.int32, (BLOCK, LANES), 1) < HEAD_DIM).astype(F32).astype(BF16)
    hi = (lax.broadcasted_iota(jnp.int32, (BLOCK, LANES), 1) >= HEAD_DIM).astype(F32).astype(BF16)
    lam = (jnp.exp(jnp.sum(lq1[...] * lk1[...], axis=-1, keepdims=True))
           - jnp.exp(jnp.sum(lq2[...] * lk2[...], axis=-1, keepdims=True)) + lam_init)

    def q_of(rows, mp):
        return q_ref[rows, :] * (lo, hi)[mp]

    def k_of(cols, mp):
        return k_ref[cols, :]

    def bias_of(delta, mp, last):
        return (bias0_ref, bias1_ref)[mp][delta]

    def finish(rows, res):
        (a0, l0), (a1, l1) = res
        o = a0 * (1.0 / l0) - a1 * (lam / l1)
        o = _rms(o, subln_ref[...]) * (1.0 - lam_init)
        o_ref[rows, :] = o.astype(o_ref.dtype)

    _causal_two_maps(seq, q_of, k_of, v_ref, bias_of, s_scr, p_scr, finish)


def _diff_attn(qkv, bias, lams, subln, lam_init, batch, seq):
    nh = DIFF_HEADS
    q3 = qkv.reshape(batch, seq, -1)
    ndelta = seq // BLOCK
    vec = pl.BlockSpec((1, HEAD_DIM), lambda h, b: (0, 0))
    out = pl.pallas_call(
        functools.partial(_diff_attn_kernel, seq, lam_init),
        grid=(nh, batch),
        in_specs=[
            pl.BlockSpec((None, seq, LANES), lambda h, b: (b, 0, h)),
            pl.BlockSpec((None, seq, LANES), lambda h, b: (b, 0, nh + h)),
            pl.BlockSpec((None, seq, LANES), lambda h, b: (b, 0, 2 * nh + h)),
            pl.BlockSpec((None, ndelta, BLOCK, KV_CHUNK), lambda h, b: (h, 0, 0, 0)),
            pl.BlockSpec((None, ndelta, BLOCK, KV_CHUNK), lambda h, b: (nh + h, 0, 0, 0)),
            vec, vec, vec, vec,
            pl.BlockSpec((1, LANES), lambda h, b: (0, 0)),
        ],
        out_specs=pl.BlockSpec((None, seq, LANES), lambda h, b: (b, 0, h)),
        out_shape=jax.ShapeDtypeStruct((batch, seq, nh * LANES), BF16),
        scratch_shapes=[pltpu.VMEM((2, BLOCK, seq), F32), pltpu.VMEM((2, BLOCK, seq), BF16)],
        compiler_params=_params(("parallel", "parallel")),
    )(q3, q3, q3, bias, bias, *lams, subln)
    return out.reshape(batch * seq, nh * LANES)


def _mla_attn_kernel(seq, q_ref, k_ref, v_ref, mask_ref, o_ref, s_scr, p_scr):
    def q_of(rows, mp):
        return q_ref[rows, mp * LANES:(mp + 1) * LANES]

    def k_of(cols, mp):
        return k_ref[cols, mp * LANES:(mp + 1) * LANES]

    def bias_of(delta, mp, last):
        return mask_ref[delta] if last else None

    def finish(rows, res):
        (a0, l0), (a1, l1) = res
        o = jnp.where(_lane_lo((BLOCK, LANES)), a0 * (1.0 / l0), a1 * (1.0 / l1))
        o_ref[rows, :] = o.astype(o_ref.dtype)

    _causal_two_maps(seq, q_of, k_of, v_ref, bias_of, s_scr, p_scr, finish)


def _mla_attn(q, k, v, mask, batch, seq):
    npair = MLA_HEADS // 2
    out = pl.pallas_call(
        functools.partial(_mla_attn_kernel, seq),
        grid=(npair, batch),
        in_specs=[
            pl.BlockSpec((None, seq, 2 * LANES), lambda p, b: (b, 0, p)),
            pl.BlockSpec((None, seq, 2 * LANES), lambda p, b: (b, 0, p)),
            pl.BlockSpec((None, seq, LANES), lambda p, b: (b, 0, p)),
            pl.BlockSpec((2, BLOCK, KV_CHUNK), lambda p, b: (0, 0, 0)),
        ],
        out_specs=pl.BlockSpec((None, seq, LANES), lambda p, b: (b, 0, p)),
        out_shape=jax.ShapeDtypeStruct((batch, seq, MLA_HEADS * MLA_V), BF16),
        scratch_shapes=[pltpu.VMEM((2, BLOCK, seq), F32), pltpu.VMEM((2, BLOCK, seq), BF16)],
        compiler_params=_params(("parallel", "parallel")),
    )(q.reshape(batch, seq, -1), k.reshape(batch, seq, -1), v.reshape(batch, seq, -1), mask)
    return out.reshape(batch * seq, MLA_HEADS * MLA_V)


def _mla_proj_kernel(scale, x_ref, g_ref, w_in_ref, qa_ref, kva_ref, wq_ref, wk_ref, wv_ref,
                     gq_ref, gk_ref, cos_ref, sin_ref, q_out, k_out, v_out):
    hn = _rms(x_ref[...], g_ref[...]).astype(BF16)
    lat = _dot(hn, w_in_ref[...])
    cq = _rms(lat[:, :MLA_Q_RANK], qa_ref[...]).astype(BF16)
    ckv = _rms(lat[:, MLA_Q_RANK:MLA_Q_RANK + MLA_KV_RANK], kva_ref[...]).astype(BF16)
    kpe = lat[:, MLA_Q_RANK + MLA_KV_RANK:]
    v_out[...] = _dot(ckv, wv_ref[...]).astype(v_out.dtype)
    real = lax.broadcasted_iota(jnp.int32, (x_ref.shape[0], LANES), 1) < MLA_QK
    cos = cos_ref[...]
    sin = sin_ref[...]

    def head(y, gain):
        ms = jnp.sum(jnp.where(real, y * y, 0.0), axis=-1, keepdims=True) * (1.0 / MLA_QK)
        yn = y * lax.rsqrt(ms + RMS_EPS) * gain
        return yn * cos + pltpu.roll(yn, LANES - MLA_ROPE, 1) * sin

    for hp in range(MLA_HEADS // 2):
        cols = slice(hp * 2 * LANES, (hp + 1) * 2 * LANES)
        yq = _dot(cq, wq_ref[:, cols])
        yk = _dot(ckv, wk_ref[:, cols])
        for j in range(2):
            lanes = slice(j * LANES, (j + 1) * LANES)
            out_cols = slice((2 * hp + j) * LANES, (2 * hp + j + 1) * LANES)
            q_out[:, out_cols] = (head(yq[:, lanes], gq_ref[...]) * scale).astype(q_out.dtype)
            k_out[:, out_cols] = head(yk[:, lanes] + kpe, gk_ref[...]).astype(k_out.dtype)


def _mla_proj(x2d, g, w_in, qa, kva, wq, wk, wv, gq, gk, cos_t, sin_t, seq):
    t = x2d.shape[0]
    per_seq = seq // ROW_TILE
    nq = MLA_HEADS * LANES
    tab = pl.BlockSpec((ROW_TILE, LANES), lambda i: (i % per_seq, 0))
    return pl.pallas_call(
        functools.partial(_mla_proj_kernel, MLA_QK ** -0.5 * LOG2E),
        grid=(t // ROW_TILE,),
        in_specs=[
            pl.BlockSpec((ROW_TILE, D_MODEL), lambda i: (i, 0)),
            _const_spec((1, D_MODEL)),
            _const_spec(w_in.shape),
            _const_spec((1, MLA_Q_RANK)),
            _const_spec((1, MLA_KV_RANK)),
            _const_spec(wq.shape),
            _const_spec(wk.shape),
            _const_spec(wv.shape),
            _const_spec((1, LANES)),
            _const_spec((1, LANES)),
            tab, tab,
        ],
        out_specs=[pl.BlockSpec((ROW_TILE, nq), lambda i: (i, 0)),
                   pl.BlockSpec((ROW_TILE, nq), lambda i: (i, 0)),
                   pl.BlockSpec((ROW_TILE, MLA_HEADS * MLA_V), lambda i: (i, 0))],
        out_shape=[jax.ShapeDtypeStruct((t, nq), BF16), jax.ShapeDtypeStruct((t, nq), BF16),
                   jax.ShapeDtypeStruct((t, MLA_HEADS * MLA_V), BF16)],
        compiler_params=_params(("parallel",)),
    )(x2d, g, w_in, qa, kva, wq, wk, wv, gq, gk, cos_t, sin_t)


def _ffn_kernel(tiles_per_seq, x_ref, o_ref, wo_ref, g_ref, wup_ref, cw_ref, cb_ref, wdn_ref,
                out_ref, h_scr, act_scr, conv_scr, tail_scr):
    rows = x_ref.shape[0]
    x2 = x_ref[...] + _dot(o_ref[...], wo_ref[...])
    out_ref[...] = x2
    h_scr[...] = _rms(x2, g_ref[...]).astype(BF16)
    first = pl.program_id(0) % tiles_per_seq == 0
    for c in range(D_FF // FF_CHUNK):
        cols = slice(c * FF_CHUNK, (c + 1) * FF_CHUNK)
        gate = _dot(h_scr[...], wup_ref[:, cols])
        up = _dot(h_scr[...], wup_ref[:, D_FF + c * FF_CHUNK:D_FF + (c + 1) * FF_CHUNK])
        conv_scr[0:8, :] = jnp.where(first, 0.0, tail_scr[:, cols])
        conv_scr[8:, :] = gate
        tail_scr[:, cols] = gate[rows - 8:, :]
        conv = (cb_ref[:, cols] + cw_ref[2:3, cols] * gate
                + cw_ref[1:2, cols] * conv_scr[7:7 + rows, :]
                + cw_ref[0:1, cols] * conv_scr[6:6 + rows, :])
        act_scr[:, cols] = (conv / (1.0 + jnp.exp(-conv)) * up).astype(BF16)
    out_ref[...] += _dot(act_scr[...], wdn_ref[...])


def _ffn(x2d, o, wo, g, wup, cw, cb, wdn, seq):
    t = x2d.shape[0]
    ko = o.shape[1]
    return pl.pallas_call(
        functools.partial(_ffn_kernel, seq // ROW_TILE),
        grid=(t // ROW_TILE,),
        in_specs=[
            pl.BlockSpec((ROW_TILE, D_MODEL), lambda i: (i, 0)),
            pl.BlockSpec((ROW_TILE, ko), lambda i: (i, 0)),
            _const_spec((ko, D_MODEL)),
            _const_spec((1, D_MODEL)),
            _const_spec((D_MODEL, 2 * D_FF)),
            _const_spec((CONV_WIDTH, D_FF)),
            _const_spec((1, D_FF)),
            _const_spec((D_FF, D_MODEL)),
        ],
        out_specs=pl.BlockSpec((ROW_TILE, D_MODEL), lambda i: (i, 0)),
        out_shape=jax.ShapeDtypeStruct((t, D_MODEL), F32),
        scratch_shapes=[
            pltpu.VMEM((ROW_TILE, D_MODEL), BF16),
            pltpu.VMEM((ROW_TILE, D_FF), BF16),
            pltpu.VMEM((ROW_TILE + 8, FF_CHUNK), F32),
            pltpu.VMEM((8, D_FF), F32),
        ],
        compiler_params=_params(("arbitrary",)),
    )(x2d, o, wo, g, wup, cw, cb, wdn)


def _bias_kernel(tiles, tab_ref, out_ref):
    col = pl.program_id(0)
    max_exact = NUM_BUCKETS // 2
    base = (lax.broadcasted_iota(jnp.int32, (BLOCK, KV_CHUNK), 0)
            - lax.broadcasted_iota(jnp.int32, (BLOCK, KV_CHUNK), 1))
    for t, (delta, dilation, window) in enumerate(tiles):
        off = base + delta * BLOCK
        dist = jnp.maximum(off, 0) * dilation
        d_f = jnp.maximum(dist, 1).astype(F32)
        large = max_exact + (jnp.log(d_f * (1.0 / max_exact))
                             * ((NUM_BUCKETS - max_exact) / math.log(MAX_DISTANCE / max_exact))
                             ).astype(jnp.int32)
        bucket = jnp.where(dist < max_exact, dist, jnp.minimum(large, NUM_BUCKETS - 1))
        val = jnp.zeros((BLOCK, KV_CHUNK), F32)
        for b in range(NUM_BUCKETS):
            val = jnp.where(bucket == b, tab_ref[b, col], val)
        valid = off >= 0 if window is None else (off >= 0) & (off <= window)
        out_ref[t] = jnp.where(valid, val * LOG2E, NEG)


def _bias_tiles(table, ncols, tiles):
    return pl.pallas_call(
        functools.partial(_bias_kernel, tiles),
        grid=(ncols,),
        in_specs=[pl.BlockSpec(memory_space=pltpu.SMEM)],
        out_specs=pl.BlockSpec((None, len(tiles), BLOCK, KV_CHUNK), lambda c: (c, 0, 0, 0)),
        out_shape=jax.ShapeDtypeStruct((ncols, len(tiles), BLOCK, KV_CHUNK), F32),
        compiler_params=_params(("parallel",)),
    )(table)


def _causal_mask(ndelta):
    delta = jnp.arange(ndelta)[:, None, None] * BLOCK
    dist = delta + jnp.arange(BLOCK)[None, :, None] - jnp.arange(KV_CHUNK)[None, None, :]
    return jnp.where(dist >= 0, 0.0, NEG).astype(F32)


def _seg_ones():
    i = jnp.arange(MXU_DIM)
    return ((i[:, None] // HEAD_DIM == i[None, :] // HEAD_DIM).astype(F32) / HEAD_DIM).astype(BF16)


def kernel(x, rel_bias_table, norm_mix, norm_ffn, a_w_in, a_q_norm, a_k_norm, a_w_out, b_w_in, b_q_a_norm, b_kv_a_norm, b_w_q_up, b_w_kv_up, b_q_norm, b_k_norm, b_w_out, c_w_in, c_q_norm, c_k_norm, c_lambda_q1, c_lambda_k1, c_lambda_q2, c_lambda_k2, c_subln, c_w_out, d_w_in, d_q_norm, d_k_norm, d_sinks, d_w_out, f_w_up, f_conv_w, f_conv_b, f_w_down):
    batch, seq, _ = x.shape
    depth = norm_mix.shape[0]
    assert seq == MAX_DISTANCE and seq % ROW_TILE == 0
    table = rel_bias_table.astype(F32)
    bd = _seg_ones()
    qk_scale = HEAD_DIM ** -0.5 * LOG2E
    row = lambda v: v.reshape(1, -1).astype(F32)
    xf = x.reshape(batch * seq, D_MODEL)

    for i in range(depth):
        m, j = i % 4, i // 4
        g_mix = row(norm_mix[i])
        if m == 0:
            hw = A_HEADS * HEAD_DIM
            gain = jnp.concatenate(
                [jnp.concatenate([jnp.tile(a_q_norm[j, g], A_HEADS) * qk_scale,
                                  jnp.tile(a_k_norm[j, g], A_HEADS),
                                  jnp.ones((hw,), F32)]) for g in range(N_DIL)])
            segs = []
            for g in range(N_DIL):
                for t in range(3):
                    for half in range(hw // MXU_DIM):
                        c0 = (g * 3 + t) * hw + half * MXU_DIM
                        oc = c0 if g == 0 else c0 - 3 * hw
                        segs.append((c0, MXU_DIM, t < 2, 0 if g == 0 else 1, oc))
            qkv0, qkv12 = _proj(xf, g_mix, a_w_in[j].astype(BF16), row(gain), bd, tuple(segs),
                                ((3 * hw, BF16), (6 * hw, F32)))
            bias = _bias_tiles(table, A_HEADS,
                               tuple((1, dil, window // dil) for window, dil in DIL_GROUPS))
            o = _dil_attn(qkv0, qkv12, bias, batch, seq)
            w_out = a_w_out[j]
        elif m == 1:
            w_in = b_w_in[j]
            lat_w = MLA_Q_RANK + MLA_KV_RANK
            pe = w_in[:, lat_w:]
            half = MLA_ROPE // 2
            swap = lambda a: jnp.concatenate([a[..., half:], a[..., :half]], axis=-1)
            w_in_ext = jnp.concatenate(
                [w_in[:, :lat_w], jnp.zeros((D_MODEL, MLA_NOPE), F32), pe, swap(pe)], axis=1)
            wq = b_w_q_up[j].reshape(MLA_Q_RANK, MLA_HEADS, MLA_QK)
            wq = jnp.concatenate([wq, swap(wq[..., MLA_NOPE:])], axis=-1).reshape(MLA_Q_RANK, -1)
            wkv = b_w_kv_up[j].reshape(MLA_KV_RANK, MLA_HEADS, MLA_NOPE + MLA_V)
            wk = jnp.concatenate([wkv[..., :MLA_NOPE], jnp.zeros_like(wkv[..., :MLA_NOPE])],
                                 axis=-1).reshape(MLA_KV_RANK, -1)
            wv = wkv[..., MLA_NOPE:].reshape(MLA_KV_RANK, -1)
            pad_gain = lambda gn: row(jnp.concatenate([gn, swap(gn[MLA_NOPE:])]))
            inv_freq = ROPE_THETA ** (-jnp.arange(0, MLA_ROPE, 2, dtype=F32) / MLA_ROPE)
            ang = jnp.arange(seq, dtype=F32)[:, None] * inv_freq[None, :]
            cos, sin = jnp.cos(ang), jnp.sin(ang)
            ones = jnp.ones((seq, MLA_NOPE), F32)
            zeros = jnp.zeros((seq, MLA_ROPE), F32)
            cos_t = jnp.concatenate([ones, cos, cos, zeros], axis=1)
            sin_t = jnp.concatenate([0.0 * ones, -sin, sin, zeros], axis=1)
            q, k, v = _mla_proj(xf, g_mix, w_in_ext.astype(BF16), row(b_q_a_norm[j]),
                                row(b_kv_a_norm[j]), wq.astype(BF16), wk.astype(BF16),
                                wv.astype(BF16), pad_gain(b_q_norm[j]), pad_gain(b_k_norm[j]),
                                cos_t, sin_t, seq)
            o = _mla_attn(q, k, v, _causal_mask(KV_CHUNK // BLOCK), batch, seq)
            w_out = b_w_out[j]
        elif m == 2:
            qk_w = DIFF_HEADS * 2 * HEAD_DIM
            gain = jnp.concatenate([jnp.tile(c_q_norm[j], 2 * DIFF_HEADS) * qk_scale,
                                    jnp.tile(c_k_norm[j], 2 * DIFF_HEADS), jnp.ones((qk_w,), F32)])
            segs = tuple((c0, MXU_DIM, c0 < 2 * qk_w, 0, c0) for c0 in range(0, 3 * qk_w, MXU_DIM))
            (qkv,) = _proj(xf, g_mix, c_w_in[j].astype(BF16), row(gain), bd, segs,
                           ((3 * qk_w, BF16),))
            bias = _bias_tiles(table, 2 * DIFF_HEADS,
                               tuple((d, 1, None) for d in range(seq // BLOCK)))
            lam_init = 0.8 - 0.6 * math.exp(-0.3 * i)
            lams = [row(v) for v in (c_lambda_q1[j], c_lambda_k1[j], c_lambda_q2[j], c_lambda_k2[j])]
            o = _diff_attn(qkv, bias, lams, row(c_subln[j]), lam_init, batch, seq)
            w_out = c_w_out[j]
        else:
            npair = SWA_Q_HEADS // 2
            qw = SWA_Q_HEADS * HEAD_DIM
            order = jnp.arange(SWA_Q_HEADS).reshape(2, npair).T.reshape(-1)
            w_in = d_w_in[j]
            wq = w_in[:, :qw].reshape(D_MODEL, SWA_Q_HEADS, HEAD_DIM)[:, order].reshape(D_MODEL, qw)
            w_perm = jnp.concatenate([wq, w_in[:, qw:]], axis=1)
            kvw = SWA_KV_HEADS * HEAD_DIM
            gain = jnp.concatenate([jnp.tile(d_q_norm[j], SWA_Q_HEADS) * qk_scale,
                                    jnp.tile(d_k_norm[j], SWA_KV_HEADS), jnp.ones((kvw,), F32)])
            segs = tuple((c0, MXU_DIM, True, 0, c0) for c0 in range(0, qw, MXU_DIM))
            segs += ((qw, kvw, True, 0, qw), (qw + kvw, kvw, False, 0, qw + kvw))
            (qkv,) = _proj(xf, g_mix, w_perm.astype(BF16), row(gain), bd, segs,
                           ((qw + 2 * kvw, BF16),))
            bias = _bias_tiles(table, SWA_Q_HEADS, ((1, 1, SWA_WINDOW - 1),))
            o = _swa_attn(qkv, bias, d_sinks[j].astype(F32), batch, seq)
            w_out = d_w_out[j].reshape(SWA_Q_HEADS, HEAD_DIM, D_MODEL)[order].reshape(qw, D_MODEL)
        xf = _ffn(xf, o, w_out.astype(BF16), row(norm_ffn[i]), f_w_up[i].astype(BF16),
                  f_conv_w[i].astype(F32), row(f_conv_b[i]), f_w_down[i].astype(BF16), seq)
    return xf.reshape(batch, seq, D_MODEL)
```

```python
import functools
import math

import jax
import jax.numpy as jnp
from jax import lax
from jax.experimental import pallas as pl
from jax.experimental.pallas import tpu as pltpu

F32 = jnp.float32
BF16 = jnp.bfloat16

D_MODEL = 1024
HEAD_DIM = 64
BLOCK = 128
RMS_EPS = 1e-6
NUM_BUCKETS = 32
MAX_DISTANCE = 2048
DIL_GROUPS = ((128, 1), (512, 4), (2048, 16))
N_DIL = 3
A_HEADS = 8
MLA_HEADS = 16
MLA_NOPE = 64
MLA_ROPE = 32
MLA_QK = 96
MLA_V = 64
MLA_Q_RANK = 384
MLA_KV_RANK = 256
ROPE_THETA = 10000.0
DIFF_HEADS = 8
SWA_Q_HEADS = 16
SWA_KV_HEADS = 2
SWA_WINDOW = 128
D_FF = 2816
CONV_WIDTH = 3

LANES = 128
MXU_DIM = 256
NEG = -1e30
LOG2E = math.log2(math.e)
ROW_TILE = 512
FF_CHUNK = 256
KV_CHUNK = 256
VMEM_LIMIT = 56 * 1024 * 1024


def _dot(a, b):
    return jnp.dot(a, b, preferred_element_type=F32)


def _dot_nt(a, b):
    return lax.dot_general(a, b, (((1,), (1,)), ((), ())), preferred_element_type=F32)


def _rms(x, g):
    ms = jnp.mean(x * x, axis=-1, keepdims=True)
    return x * lax.rsqrt(ms + RMS_EPS) * g


def _const_spec(shape):
    nd = len(shape)
    return pl.BlockSpec(shape, lambda *_: (0,) * nd, pipeline_mode=pl.Buffered(1))


def _params(sem):
    return pltpu.CompilerParams(dimension_semantics=sem, vmem_limit_bytes=VMEM_LIMIT)


def _lane_lo(shape):
    return lax.broadcasted_iota(jnp.int32, shape, len(shape) - 1) < HEAD_DIM


def _seg_norm(y, bd_ref):
    w = y.shape[-1]
    ms = _dot((y * y).astype(BF16), bd_ref[:w, :w])
    return y * lax.rsqrt(ms + RMS_EPS)


def _proj_kernel(segs, x_ref, g_ref, w_ref, gain_ref, bd_ref, *out_refs):
    hn = _rms(x_ref[...], g_ref[...]).astype(BF16)
    for (c0, width, norm, oi, oc) in segs:
        y = _dot(hn, w_ref[:, c0:c0 + width])
        if norm:
            y = _seg_norm(y, bd_ref) * gain_ref[:, c0:c0 + width]
        out_refs[oi][:, oc:oc + width] = y.astype(out_refs[oi].dtype)


def _proj(x2d, g, w, gain, bd, segs, outs):
    t = x2d.shape[0]
    n = w.shape[1]
    return pl.pallas_call(
        functools.partial(_proj_kernel, segs),
        grid=(t // ROW_TILE,),
        in_specs=[
            pl.BlockSpec((ROW_TILE, D_MODEL), lambda i: (i, 0)),
            _const_spec((1, D_MODEL)),
            _const_spec((D_MODEL, n)),
            _const_spec((1, n)),
            _const_spec((MXU_DIM, MXU_DIM)),
        ],
        out_specs=[pl.BlockSpec((ROW_TILE, ow), lambda i: (i, 0)) for ow, _ in outs],
        out_shape=[jax.ShapeDtypeStruct((t, ow), od) for ow, od in outs],
        compiler_params=_params(("parallel",)),
    )(x2d, g, w, gain, bd)


def _band_unit(q_lo, q_hi, segs, sinks=None):
    res = []
    for h, qh in enumerate((q_lo, q_hi)):
        ss = [_dot_nt(qh, k) + (b_lo, b_hi)[h] for (k, _, b_lo, b_hi) in segs]
        m = jnp.max(functools.reduce(jnp.maximum, ss), axis=-1, keepdims=True)
        if sinks is not None:
            m = jnp.maximum(m, sinks[h])
        ps = [jnp.exp2(s - m) for s in ss]
        l = jnp.sum(functools.reduce(jnp.add, ps), axis=-1, keepdims=True)
        if sinks is not None:
            l = l + jnp.exp2(sinks[h] - m)
        acc = sum(_dot(p.astype(BF16), v) for p, (_, v, _, _) in zip(ps, segs))
        res.append((acc, m, l))
    lo = _lane_lo((BLOCK, LANES))
    (a0, m0, l0), (a1, m1, l1) = res
    return (jnp.where(lo, a0, a1),
            jnp.where(lo, jnp.broadcast_to(m0, (BLOCK, LANES)), jnp.broadcast_to(m1, (BLOCK, LANES))),
            jnp.where(lo, jnp.broadcast_to(l0, (BLOCK, LANES)), jnp.broadcast_to(l1, (BLOCK, LANES))))


def _split_pair(q):
    lo = _lane_lo(q.shape)
    zero = jnp.zeros_like(q)
    return jnp.where(lo, q, zero).astype(BF16), jnp.where(lo, zero, q).astype(BF16)


def _dil_attn_kernel(seq, q0, k0, v0, q1, k1, v1, q2, k2, v2, bias_ref, o_ref,
                     perm_scr, acc_scr, m_scr, l_scr, mrg_scr, nat_scr):
    nblk = seq // BLOCK
    dils = [dil for _, dil in DIL_GROUPS]

    for g, srcs in ((1, (q1, k1, v1)), (2, (q2, k2, v2))):
        dil = dils[g]
        for u in range(nblk):
            r, j = divmod(u, nblk // dil)
            vals = [ref[pl.ds(r + dil * BLOCK * j, BLOCK, stride=dil), :] for ref in srcs]
            for t, val in enumerate(vals):
                perm_scr[(g - 1) * 3 + t, pl.ds(u * BLOCK, BLOCK), :] = val.astype(BF16)

    for g, dil in enumerate(dils):
        if g == 0:
            qr, kr, vr = q0, k0, v0
        else:
            qr, kr, vr = (perm_scr.at[(g - 1) * 3 + t] for t in range(3))
        for u in range(nblk):
            rows = pl.ds(u * BLOCK, BLOCK)
            has_prev = u % (nblk // dil) != 0
            krows = pl.ds((u - 1) * BLOCK, 2 * BLOCK) if has_prev else rows
            lo = 0 if has_prev else BLOCK
            q_lo, q_hi = _split_pair(qr[rows, :].astype(F32))
            acc, m, l = _band_unit(q_lo, q_hi, [(kr[krows, :], vr[krows, :],
                                                 bias_ref[0, g, :, lo:], bias_ref[1, g, :, lo:])])
            acc_scr[g, rows, :] = acc
            m_scr[g, rows, :] = m
            l_scr[g, rows, :] = l

    d_last = dils[-1]
    for r in range(d_last):
        def rows_in(g):
            dil = dils[g]
            start = (r % dil) * (seq // dil) + r // dil
            step = d_last // dil
            return pl.ds(start, BLOCK) if step == 1 else pl.ds(start, BLOCK, stride=step)

        ms = [m_scr[g, rows_in(g), :] for g in range(N_DIL)]
        mm = functools.reduce(jnp.maximum, ms)
        ws = [jnp.exp2(m - mm) for m in ms]
        num = sum(w * acc_scr[g, rows_in(g), :] for g, w in enumerate(ws))
        den = sum(w * l_scr[g, rows_in(g), :] for g, w in enumerate(ws))
        mrg_scr[pl.ds(r * BLOCK, BLOCK), :] = num / den

    for r in range(d_last):
        nat_scr[pl.ds(r, BLOCK, stride=d_last), :] = mrg_scr[pl.ds(r * BLOCK, BLOCK), :]
    o_ref[...] = nat_scr[...].astype(o_ref.dtype)


def _dil_attn(qkv0, qkv12, bias, batch, seq):
    npair = A_HEADS // 2
    nh = npair
    q3 = qkv0.reshape(batch, seq, -1)
    f3 = qkv12.reshape(batch, seq, -1)

    def spec(off):
        return pl.BlockSpec((None, seq, LANES), lambda b, p, off=off: (b, 0, off + p))

    in_specs = ([spec(0), spec(nh), spec(2 * nh)]
                + [spec(0), spec(nh), spec(2 * nh)]
                + [spec(3 * nh), spec(4 * nh), spec(5 * nh)]
                + [pl.BlockSpec((2, N_DIL, BLOCK, 2 * BLOCK), lambda b, p: (p, 0, 0, 0))])
    out = pl.pallas_call(
        functools.partial(_dil_attn_kernel, seq),
        grid=(batch, npair),
        in_specs=in_specs,
        out_specs=pl.BlockSpec((None, seq, LANES), lambda b, p: (b, 0, p)),
        out_shape=jax.ShapeDtypeStruct((batch, seq, A_HEADS * HEAD_DIM), BF16),
        scratch_shapes=([pltpu.VMEM((3 * (N_DIL - 1), seq, LANES), BF16)]
                        + [pltpu.VMEM((N_DIL, seq, LANES), F32)] * 3
                        + [pltpu.VMEM((seq, LANES), F32)] * 2),
        compiler_params=_params(("parallel", "parallel")),
    )(q3, q3, q3, f3, f3, f3, f3, f3, f3, bias)
    return out.reshape(batch * seq, A_HEADS * HEAD_DIM)


def _swa_attn_kernel(seq, sinks_ref, q_ref, k_ref, v_ref, blo_ref, bhi_ref, o_ref):
    p = pl.program_id(1)
    sinks = (sinks_ref[p] * LOG2E, sinks_ref[p + SWA_Q_HEADS // 2] * LOG2E)
    for j in range(seq // BLOCK):
        qrows = pl.ds(j * BLOCK, BLOCK)
        krows = pl.ds((j - 1) * BLOCK, 2 * BLOCK) if j > 0 else qrows
        lo = 0 if j > 0 else BLOCK
        q_lo, q_hi = _split_pair(q_ref[qrows, :].astype(F32))
        acc, _, l = _band_unit(q_lo, q_hi, [(k_ref[krows, :], v_ref[krows, :],
                                             blo_ref[0, :, lo:], bhi_ref[0, :, lo:])], sinks)
        o_ref[qrows, :] = (acc / l).astype(o_ref.dtype)


def _swa_attn(qkv, bias, sinks, batch, seq):
    npair = SWA_Q_HEADS // 2
    q3 = qkv.reshape(batch, seq, -1)
    out = pl.pallas_call(
        functools.partial(_swa_attn_kernel, seq),
        grid=(batch, npair),
        in_specs=[
            pl.BlockSpec(memory_space=pltpu.SMEM),
            pl.BlockSpec((None, seq, LANES), lambda b, p: (b, 0, p)),
            pl.BlockSpec((None, seq, LANES), lambda b, p: (b, 0, npair)),
            pl.BlockSpec((None, seq, LANES), lambda b, p: (b, 0, npair + 1)),
            pl.BlockSpec((None, 1, BLOCK, 2 * BLOCK), lambda b, p: (p, 0, 0, 0)),
            pl.BlockSpec((None, 1, BLOCK, 2 * BLOCK), lambda b, p: (p + npair, 0, 0, 0)),
        ],
        out_specs=pl.BlockSpec((None, seq, LANES), lambda b, p: (b, 0, p)),
        out_shape=jax.ShapeDtypeStruct((batch, seq, SWA_Q_HEADS * HEAD_DIM), BF16),
        compiler_params=_params(("parallel", "parallel")),
    )(sinks, q3, q3, q3, bias, bias)
    return out.reshape(batch * seq, SWA_Q_HEADS * HEAD_DIM)


def _causal_two_maps(seq, q_of, k_of, v_ref, bias_of, s_scr, p_scr, finish):
    nblk = seq // BLOCK
    for i in range(nblk):
        rows = slice(i * BLOCK, (i + 1) * BLOCK)
        nchunk = i * BLOCK // KV_CHUNK + 1
        width = nchunk * KV_CHUNK
        res = []
        for mp in range(2):
            qm = q_of(rows, mp)
            mx = None
            for c in range(nchunk):
                cols = slice(c * KV_CHUNK, (c + 1) * KV_CHUNK)
                s = _dot_nt(qm, k_of(cols, mp))
                b = bias_of(i - c * (KV_CHUNK // BLOCK), mp, c == nchunk - 1)
                if b is not None:
                    s = s + b
                s_scr[mp, :, cols] = s
                t = jnp.maximum(s[:, :LANES], s[:, LANES:])
                mx = t if mx is None else jnp.maximum(mx, t)
            m = jnp.max(mx, axis=-1, keepdims=True)
            ls = None
            for c in range(nchunk):
                cols = slice(c * KV_CHUNK, (c + 1) * KV_CHUNK)
                p = jnp.exp2(s_scr[mp, :, cols] - m)
                t = p[:, :LANES] + p[:, LANES:]
                ls = t if ls is None else ls + t
                p_scr[mp, :, cols] = p.astype(BF16)
            l = jnp.sum(ls, axis=-1, keepdims=True)
            acc = _dot(p_scr[mp, :, :width], v_ref[:width, :])
            res.append((acc, l))
        finish(rows, res)


def _diff_attn_kernel(seq, lam_init, q_ref, k_ref, v_ref, bias0_ref, bias1_ref, lq1, lk1, lq2, lk2,
                      subln_ref, o_ref, s_scr, p_scr):
    lo = (lax.broadcasted_iota(jnp.int32, (BLOCK, LANES), 1) < HEAD_DIM).astype(F32).astype(BF16)
    hi = (lax.broadcasted_iota(jnp.int32, (BLOCK, LANES), 1) >= HEAD_DIM).astype(F32).astype(BF16)
    lam = (jnp.exp(jnp.sum(lq1[...] * lk1[...], axis=-1, keepdims=True))
           - jnp.exp(jnp.sum(lq2[...] * lk2[...], axis=-1, keepdims=True)) + lam_init)

    def q_of(rows, mp):
        return q_ref[rows, :] * (lo, hi)[mp]

    def k_of(cols, mp):
        return k_ref[cols, :]

    def bias_of(delta, mp, last):
        return (bias0_ref, bias1_ref)[mp][delta]

    def finish(rows, res):
        (a0, l0), (a1, l1) = res
        o = a0 * (1.0 / l0) - a1 * (lam / l1)
        o = _rms(o, subln_ref[...]) * (1.0 - lam_init)
        o_ref[rows, :] = o.astype(o_ref.dtype)

    _causal_two_maps(seq, q_of, k_of, v_ref, bias_of, s_scr, p_scr, finish)


def _diff_attn(qkv, bias, lams, subln, lam_init, batch, seq):
    nh = DIFF_HEADS
    q3 = qkv.reshape(batch, seq, -1)
    ndelta = seq // BLOCK
    vec = pl.BlockSpec((1, HEAD_DIM), lambda h, b: (0, 0))
    out = pl.pallas_call(
        functools.partial(_diff_attn_kernel, seq, lam_init),
        grid=(nh, batch),
        in_specs=[
            pl.BlockSpec((None, seq, LANES), lambda h, b: (b, 0, h)),
            pl.BlockSpec((None, seq, LANES), lambda h, b: (b, 0, nh + h)),
            pl.BlockSpec((None, seq, LANES), lambda h, b: (b, 0, 2 * nh + h)),
            pl.BlockSpec((None, ndelta, BLOCK, KV_CHUNK), lambda h, b: (h, 0, 0, 0)),
            pl.BlockSpec((None, ndelta, BLOCK, KV_CHUNK), lambda h, b: (nh + h, 0, 0, 0)),
            vec, vec, vec, vec,
            pl.BlockSpec((1, LANES), lambda h, b: (0, 0)),
        ],
        out_specs=pl.BlockSpec((None, seq, LANES), lambda h, b: (b, 0, h)),
        out_shape=jax.ShapeDtypeStruct((batch, seq, nh * LANES), BF16),
        scratch_shapes=[pltpu.VMEM((2, BLOCK, seq), F32), pltpu.VMEM((2, BLOCK, seq), BF16)],
        compiler_params=_params(("parallel", "parallel")),
    )(q3, q3, q3, bias, bias, *lams, subln)
    return out.reshape(batch * seq, nh * LANES)


def _mla_attn_kernel(seq, q_ref, k_ref, v_ref, mask_ref, o_ref, s_scr, p_scr):
    def q_of(rows, mp):
        return q_ref[rows, mp * LANES:(mp + 1) * LANES]

    def k_of(cols, mp):
        return k_ref[cols, mp * LANES:(mp + 1) * LANES]

    def bias_of(delta, mp, last):
        return mask_ref[delta] if last else None

    def finish(rows, res):
        (a0, l0), (a1, l1) = res
        o = jnp.where(_lane_lo((BLOCK, LANES)), a0 * (1.0 / l0), a1 * (1.0 / l1))
        o_ref[rows, :] = o.astype(o_ref.dtype)

    _causal_two_maps(seq, q_of, k_of, v_ref, bias_of, s_scr, p_scr, finish)


def _mla_attn(q, k, v, mask, batch, seq):
    npair = MLA_HEADS // 2
    out = pl.pallas_call(
        functools.partial(_mla_attn_kernel, seq),
        grid=(npair, batch),
        in_specs=[
            pl.BlockSpec((None, seq, 2 * LANES), lambda p, b: (b, 0, p)),
            pl.BlockSpec((None, seq, 2 * LANES), lambda p, b: (b, 0, p)),
            pl.BlockSpec((None, seq, LANES), lambda p, b: (b, 0, p)),
            pl.BlockSpec((2, BLOCK, KV_CHUNK), lambda p, b: (0, 0, 0)),
        ],
        out_specs=pl.BlockSpec((None, seq, LANES), lambda p, b: (b, 0, p)),
        out_shape=jax.ShapeDtypeStruct((batch, seq, MLA_HEADS * MLA_V), BF16),
        scratch_shapes=[pltpu.VMEM((2, BLOCK, seq), F32), pltpu.VMEM((2, BLOCK, seq), BF16)],
        compiler_params=_params(("parallel", "parallel")),
    )(q.reshape(batch, seq, -1), k.reshape(batch, seq, -1), v.reshape(batch, seq, -1), mask)
    return out.reshape(batch * seq, MLA_HEADS * MLA_V)


def _mla_proj_kernel(x_ref, g_ref, w_in_ref, qa_ref, kva_ref, wq_ref, wk_ref, wv_ref, ones_ref,
                     cosq_ref, sinq_ref, cosk_ref, sink_ref, q_out, k_out, v_out):
    hn = _rms(x_ref[...], g_ref[...]).astype(BF16)
    lat = _dot(hn, w_in_ref[...])
    cq = _rms(lat[:, :MLA_Q_RANK], qa_ref[...]).astype(BF16)
    ckv = _rms(lat[:, MLA_Q_RANK:MLA_Q_RANK + MLA_KV_RANK], kva_ref[...]).astype(BF16)
    kpe = lat[:, MLA_Q_RANK + MLA_KV_RANK:]
    kpe2 = jnp.concatenate([kpe, kpe], axis=1)
    v_out[...] = _dot(ckv, wv_ref[...]).astype(v_out.dtype)

    def heads(y, cos_ref, sin_ref):
        ms = _dot((y * y).astype(BF16), ones_ref[...]) * (1.0 / MLA_QK)
        t = y * cos_ref[...] + pltpu.roll(y, 2 * LANES - MLA_ROPE, 1) * sin_ref[...]
        return t * lax.rsqrt(ms + RMS_EPS)

    for hp in range(MLA_HEADS // 2):
        cols = slice(hp * 2 * LANES, (hp + 1) * 2 * LANES)
        q_out[:, cols] = heads(_dot(cq, wq_ref[:, cols]), cosq_ref, sinq_ref).astype(q_out.dtype)
        k_out[:, cols] = heads(_dot(ckv, wk_ref[:, cols]) + kpe2, cosk_ref, sink_ref).astype(k_out.dtype)


def _mla_proj(x2d, g, w_in, qa, kva, wq, wk, wv, ones, tables, seq):
    t = x2d.shape[0]
    per_seq = seq // ROW_TILE
    nq = MLA_HEADS * LANES
    tab = pl.BlockSpec((ROW_TILE, 2 * LANES), lambda i: (i % per_seq, 0))
    return pl.pallas_call(
        _mla_proj_kernel,
        grid=(t // ROW_TILE,),
        in_specs=[
            pl.BlockSpec((ROW_TILE, D_MODEL), lambda i: (i, 0)),
            _const_spec((1, D_MODEL)),
            _const_spec(w_in.shape),
            _const_spec((1, MLA_Q_RANK)),
            _const_spec((1, MLA_KV_RANK)),
            _const_spec(wq.shape),
            _const_spec(wk.shape),
            _const_spec(wv.shape),
            _const_spec((MXU_DIM, MXU_DIM)),
            tab, tab, tab, tab,
        ],
        out_specs=[pl.BlockSpec((ROW_TILE, nq), lambda i: (i, 0)),
                   pl.BlockSpec((ROW_TILE, nq), lambda i: (i, 0)),
                   pl.BlockSpec((ROW_TILE, MLA_HEADS * MLA_V), lambda i: (i, 0))],
        out_shape=[jax.ShapeDtypeStruct((t, nq), BF16), jax.ShapeDtypeStruct((t, nq), BF16),
                   jax.ShapeDtypeStruct((t, MLA_HEADS * MLA_V), BF16)],
        compiler_params=_params(("parallel",)),
    )(x2d, g, w_in, qa, kva, wq, wk, wv, ones, *tables)


def _ffn_kernel(tiles_per_seq, x_ref, o_ref, wo_ref, g_ref, wup_ref, cw_ref, cb_ref, wdn_ref,
                out_ref, h_scr, act_scr, conv_scr, tail_scr):
    rows = x_ref.shape[0]
    x2 = x_ref[...] + _dot(o_ref[...], wo_ref[...])
    out_ref[...] = x2
    h_scr[...] = _rms(x2, g_ref[...]).astype(BF16)
    first = pl.program_id(0) % tiles_per_seq == 0
    for c in range(D_FF // FF_CHUNK):
        cols = slice(c * FF_CHUNK, (c + 1) * FF_CHUNK)
        gate = _dot(h_scr[...], wup_ref[:, cols])
        up = _dot(h_scr[...], wup_ref[:, D_FF + c * FF_CHUNK:D_FF + (c + 1) * FF_CHUNK])
        conv_scr[0:8, :] = jnp.where(first, 0.0, tail_scr[:, cols])
        conv_scr[8:, :] = gate
        tail_scr[:, cols] = gate[rows - 8:, :]
        conv = (cb_ref[:, cols] + cw_ref[2:3, cols] * gate
                + cw_ref[1:2, cols] * conv_scr[7:7 + rows, :]
                + cw_ref[0:1, cols] * conv_scr[6:6 + rows, :])
        act_scr[:, cols] = (conv / (1.0 + jnp.exp(-conv)) * up).astype(BF16)
    out_ref[...] += _dot(act_scr[...], wdn_ref[...])


def _ffn(x2d, o, wo, g, wup, cw, cb, wdn, seq):
    t = x2d.shape[0]
    ko = o.shape[1]
    return pl.pallas_call(
        functools.partial(_ffn_kernel, seq // ROW_TILE),
        grid=(t // ROW_TILE,),
        in_specs=[
            pl.BlockSpec((ROW_TILE, D_MODEL), lambda i: (i, 0)),
            pl.BlockSpec((ROW_TILE, ko), lambda i: (i, 0)),
            _const_spec((ko, D_MODEL)),
            _const_spec((1, D_MODEL)),
            _const_spec((D_MODEL, 2 * D_FF)),
            _const_spec((CONV_WIDTH, D_FF)),
            _const_spec((1, D_FF)),
            _const_spec((D_FF, D_MODEL)),
        ],
        out_specs=pl.BlockSpec((ROW_TILE, D_MODEL), lambda i: (i, 0)),
        out_shape=jax.ShapeDtypeStruct((t, D_MODEL), F32),
        scratch_shapes=[
            pltpu.VMEM((ROW_TILE, D_MODEL), BF16),
            pltpu.VMEM((ROW_TILE, D_FF), BF16),
            pltpu.VMEM((ROW_TILE + 8, FF_CHUNK), F32),
            pltpu.VMEM((8, D_FF), F32),
        ],
        compiler_params=_params(("arbitrary",)),
    )(x2d, o, wo, g, wup, cw, cb, wdn)


def _bias_kernel(tiles, tab_ref, out_ref):
    col = pl.program_id(0)
    max_exact = NUM_BUCKETS // 2
    base = (lax.broadcasted_iota(jnp.int32, (BLOCK, KV_CHUNK), 0)
            - lax.broadcasted_iota(jnp.int32, (BLOCK, KV_CHUNK), 1))
    for t, (delta, dilation, window) in enumerate(tiles):
        off = base + delta * BLOCK
        dist = jnp.maximum(off, 0) * dilation
        d_f = jnp.maximum(dist, 1).astype(F32)
        large = max_exact + (jnp.log(d_f * (1.0 / max_exact))
                             * ((NUM_BUCKETS - max_exact) / math.log(MAX_DISTANCE / max_exact))
                             ).astype(jnp.int32)
        bucket = jnp.where(dist < max_exact, dist, jnp.minimum(large, NUM_BUCKETS - 1))
        val = jnp.zeros((BLOCK, KV_CHUNK), F32)
        for b in range(NUM_BUCKETS):
            val = jnp.where(bucket == b, tab_ref[b, col], val)
        valid = off >= 0 if window is None else (off >= 0) & (off <= window)
        out_ref[t] = jnp.where(valid, val * LOG2E, NEG)


def _bias_tiles(table, ncols, tiles):
    return pl.pallas_call(
        functools.partial(_bias_kernel, tiles),
        grid=(ncols,),
        in_specs=[pl.BlockSpec(memory_space=pltpu.SMEM)],
        out_specs=pl.BlockSpec((None, len(tiles), BLOCK, KV_CHUNK), lambda c: (c, 0, 0, 0)),
        out_shape=jax.ShapeDtypeStruct((ncols, len(tiles), BLOCK, KV_CHUNK), F32),
        compiler_params=_params(("parallel",)),
    )(table)


def _causal_mask(ndelta):
    delta = jnp.arange(ndelta)[:, None, None] * BLOCK
    dist = delta + jnp.arange(BLOCK)[None, :, None] - jnp.arange(KV_CHUNK)[None, None, :]
    return jnp.where(dist >= 0, 0.0, NEG).astype(F32)


def _seg_ones():
    i = jnp.arange(MXU_DIM)
    return ((i[:, None] // HEAD_DIM == i[None, :] // HEAD_DIM).astype(F32) / HEAD_DIM).astype(BF16)


def kernel(x, rel_bias_table, norm_mix, norm_ffn, a_w_in, a_q_norm, a_k_norm, a_w_out, b_w_in, b_q_a_norm, b_kv_a_norm, b_w_q_up, b_w_kv_up, b_q_norm, b_k_norm, b_w_out, c_w_in, c_q_norm, c_k_norm, c_lambda_q1, c_lambda_k1, c_lambda_q2, c_lambda_k2, c_subln, c_w_out, d_w_in, d_q_norm, d_k_norm, d_sinks, d_w_out, f_w_up, f_conv_w, f_conv_b, f_w_down):
    batch, seq, _ = x.shape
    depth = norm_mix.shape[0]
    assert seq == MAX_DISTANCE and seq % ROW_TILE == 0
    table = rel_bias_table.astype(F32)
    bd = _seg_ones()
    qk_scale = HEAD_DIM ** -0.5 * LOG2E
    row = lambda v: v.reshape(1, -1).astype(F32)
    xf = x.reshape(batch * seq, D_MODEL)

    for i in range(depth):
        m, j = i % 4, i // 4
        g_mix = row(norm_mix[i])
        if m == 0:
            hw = A_HEADS * HEAD_DIM
            gain = jnp.concatenate(
                [jnp.concatenate([jnp.tile(a_q_norm[j, g], A_HEADS) * qk_scale,
                                  jnp.tile(a_k_norm[j, g], A_HEADS),
                                  jnp.ones((hw,), F32)]) for g in range(N_DIL)])
            segs = []
            for g in range(N_DIL):
                for t in range(3):
                    for half in range(hw // MXU_DIM):
                        c0 = (g * 3 + t) * hw + half * MXU_DIM
                        oc = c0 if g == 0 else c0 - 3 * hw
                        segs.append((c0, MXU_DIM, t < 2, 0 if g == 0 else 1, oc))
            qkv0, qkv12 = _proj(xf, g_mix, a_w_in[j].astype(BF16), row(gain), bd, tuple(segs),
                                ((3 * hw, BF16), (6 * hw, F32)))
            bias = _bias_tiles(table, A_HEADS,
                               tuple((1, dil, window // dil) for window, dil in DIL_GROUPS))
            o = _dil_attn(qkv0, qkv12, bias, batch, seq)
            w_out = a_w_out[j]
        elif m == 1:
            w_in = b_w_in[j]
            lat_w = MLA_Q_RANK + MLA_KV_RANK
            pe = w_in[:, lat_w:]
            half = MLA_ROPE // 2
            swap = lambda a: jnp.concatenate([a[..., half:], a[..., :half]], axis=-1)
            w_in_ext = jnp.concatenate(
                [w_in[:, :lat_w], jnp.zeros((D_MODEL, MLA_NOPE), F32), pe, swap(pe)], axis=1)
            wq = b_w_q_up[j].reshape(MLA_Q_RANK, MLA_HEADS, MLA_QK)
            wq = jnp.concatenate([wq, swap(wq[..., MLA_NOPE:])], axis=-1).reshape(MLA_Q_RANK, -1)
            wkv = b_w_kv_up[j].reshape(MLA_KV_RANK, MLA_HEADS, MLA_NOPE + MLA_V)
            wk = jnp.concatenate([wkv[..., :MLA_NOPE], jnp.zeros_like(wkv[..., :MLA_NOPE])],
                                 axis=-1).reshape(MLA_KV_RANK, -1)
            wv = wkv[..., MLA_NOPE:].reshape(MLA_KV_RANK, -1)
            inv_freq = ROPE_THETA ** (-jnp.arange(0, MLA_ROPE, 2, dtype=F32) / MLA_ROPE)
            ang = jnp.arange(seq, dtype=F32)[:, None] * inv_freq[None, :]
            cos, sin = jnp.cos(ang), jnp.sin(ang)
            ones = jnp.ones((seq, MLA_NOPE), F32)
            zeros = jnp.zeros((seq, MLA_ROPE), F32)
            cos_t = jnp.concatenate([ones, cos, cos, zeros], axis=1)
            sin_t = jnp.concatenate([0.0 * ones, -sin, sin, zeros], axis=1)

            def rope_tables(gn, scale):
                gpad = jnp.concatenate([gn, swap(gn[MLA_NOPE:])]).astype(F32)
                cos_g = cos_t * gpad[None, :] * scale
                sin_g = sin_t * jnp.roll(gpad, -MLA_ROPE)[None, :] * scale
                return jnp.tile(cos_g, (1, 2)), jnp.tile(sin_g, (1, 2))

            lane = jnp.arange(MXU_DIM)
            real_sum = ((lane[:, None] // LANES == lane[None, :] // LANES)
                        & (lane[:, None] % LANES < MLA_QK)).astype(BF16)
            tables = (rope_tables(b_q_norm[j], MLA_QK ** -0.5 * LOG2E)
                      + rope_tables(b_k_norm[j], 1.0))
            q, k, v = _mla_proj(xf, g_mix, w_in_ext.astype(BF16), row(b_q_a_norm[j]),
                                row(b_kv_a_norm[j]), wq.astype(BF16), wk.astype(BF16),
                                wv.astype(BF16), real_sum, tables, seq)
            o = _mla_attn(q, k, v, _causal_mask(KV_CHUNK // BLOCK), batch, seq)
            w_out = b_w_out[j]
        elif m == 2:
            qk_w = DIFF_HEADS * 2 * HEAD_DIM
            gain = jnp.concatenate([jnp.tile(c_q_norm[j], 2 * DIFF_HEADS) * qk_scale,
                                    jnp.tile(c_k_norm[j], 2 * DIFF_HEADS), jnp.ones((qk_w,), F32)])
            segs = tuple((c0, MXU_DIM, c0 < 2 * qk_w, 0, c0) for c0 in range(0, 3 * qk_w, MXU_DIM))
            (qkv,) = _proj(xf, g_mix, c_w_in[j].astype(BF16), row(gain), bd, segs,
                           ((3 * qk_w, BF16),))
            bias = _bias_tiles(table, 2 * DIFF_HEADS,
                               tuple((d, 1, None) for d in range(seq // BLOCK)))
            lam_init = 0.8 - 0.6 * math.exp(-0.3 * i)
            lams = [row(v) for v in (c_lambda_q1[j], c_lambda_k1[j], c_lambda_q2[j], c_lambda_k2[j])]
            o = _diff_attn(qkv, bias, lams, row(c_subln[j]), lam_init, batch, seq)
            w_out = c_w_out[j]
        else:
            npair = SWA_Q_HEADS // 2
            qw = SWA_Q_HEADS * HEAD_DIM
            order = jnp.arange(SWA_Q_HEADS).reshape(2, npair).T.reshape(-1)
            w_in = d_w_in[j]
            wq = w_in[:, :qw].reshape(D_MODEL, SWA_Q_HEADS, HEAD_DIM)[:, order].reshape(D_MODEL, qw)
            w_perm = jnp.concatenate([wq, w_in[:, qw:]], axis=1)
            kvw = SWA_KV_HEADS * HEAD_DIM
            gain = jnp.concatenate([jnp.tile(d_q_norm[j], SWA_Q_HEADS) * qk_scale,
                                    jnp.tile(d_k_norm[j], SWA_KV_HEADS), jnp.ones((kvw,), F32)])
            segs = tuple((c0, MXU_DIM, True, 0, c0) for c0 in range(0, qw, MXU_DIM))
            segs += ((qw, kvw, True, 0, qw), (qw + kvw, kvw, False, 0, qw + kvw))
            (qkv,) = _proj(xf, g_mix, w_perm.astype(BF16), row(gain), bd, segs,
                           ((qw + 2 * kvw, BF16),))
            bias = _bias_tiles(table, SWA_Q_HEADS, ((1, 1, SWA_WINDOW - 1),))
            o = _swa_attn(qkv, bias, d_sinks[j].astype(F32), batch, seq)
            w_out = d_w_out[j].reshape(SWA_Q_HEADS, HEAD_DIM, D_MODEL)[order].reshape(qw, D_MODEL)
        xf = _ffn(xf, o, w_out.astype(BF16), row(norm_ffn[i]), f_w_up[i].astype(BF16),
                  f_conv_w[i].astype(F32), row(f_conv_b[i]), f_w_down[i].astype(BF16), seq)
    return xf.reshape(batch, seq, D_MODEL)
```

```python
import functools
import math

import jax
import jax.numpy as jnp
from jax import lax
from jax.experimental import pallas as pl
from jax.experimental.pallas import tpu as pltpu

F32 = jnp.float32
BF16 = jnp.bfloat16

D_MODEL = 1024
HEAD_DIM = 64
BLOCK = 128
RMS_EPS = 1e-6
NUM_BUCKETS = 32
MAX_DISTANCE = 2048
DIL_GROUPS = ((128, 1), (512, 4), (2048, 16))
N_DIL = 3
A_HEADS = 8
MLA_HEADS = 16
MLA_NOPE = 64
MLA_ROPE = 32
MLA_QK = 96
MLA_V = 64
MLA_Q_RANK = 384
MLA_KV_RANK = 256
ROPE_THETA = 10000.0
DIFF_HEADS = 8
SWA_Q_HEADS = 16
SWA_KV_HEADS = 2
SWA_WINDOW = 128
D_FF = 2816
CONV_WIDTH = 3

LANES = 128
MXU_DIM = 256
NEG = -1e30
LOG2E = math.log2(math.e)
ROW_TILE = 512
FF_CHUNK = 256
KV_CHUNK = 256
Q_TILE = 256
VMEM_LIMIT = 56 * 1024 * 1024


def _dot(a, b):
    return jnp.dot(a, b, preferred_element_type=F32)


def _dot_nt(a, b):
    return lax.dot_general(a, b, (((1,), (1,)), ((), ())), preferred_element_type=F32)


def _rms(x, g):
    ms = jnp.mean(x * x, axis=-1, keepdims=True)
    return x * lax.rsqrt(ms + RMS_EPS) * g


def _const_spec(shape):
    nd = len(shape)
    return pl.BlockSpec(shape, lambda *_: (0,) * nd, pipeline_mode=pl.Buffered(1))


def _params(sem):
    return pltpu.CompilerParams(dimension_semantics=sem, vmem_limit_bytes=VMEM_LIMIT)


def _lane_lo(shape):
    return lax.broadcasted_iota(jnp.int32, shape, len(shape) - 1) < HEAD_DIM


def _seg_norm(y, bd_ref):
    w = y.shape[-1]
    ms = _dot((y * y).astype(BF16), bd_ref[:w, :w])
    return y * lax.rsqrt(ms + RMS_EPS)


def _proj_kernel(segs, x_ref, g_ref, w_ref, gain_ref, bd_ref, *out_refs):
    hn = _rms(x_ref[...], g_ref[...]).astype(BF16)
    for (c0, width, norm, oi, oc) in segs:
        y = _dot(hn, w_ref[:, c0:c0 + width])
        if norm:
            y = _seg_norm(y, bd_ref) * gain_ref[:, c0:c0 + width]
        out_refs[oi][:, oc:oc + width] = y.astype(out_refs[oi].dtype)


def _proj(x2d, g, w, gain, bd, segs, outs):
    t = x2d.shape[0]
    n = w.shape[1]
    return pl.pallas_call(
        functools.partial(_proj_kernel, segs),
        grid=(t // ROW_TILE,),
        in_specs=[
            pl.BlockSpec((ROW_TILE, D_MODEL), lambda i: (i, 0)),
            _const_spec((1, D_MODEL)),
            _const_spec((D_MODEL, n)),
            _const_spec((1, n)),
            _const_spec((MXU_DIM, MXU_DIM)),
        ],
        out_specs=[pl.BlockSpec((ROW_TILE, ow), lambda i: (i, 0)) for ow, _ in outs],
        out_shape=[jax.ShapeDtypeStruct((t, ow), od) for ow, od in outs],
        compiler_params=_params(("parallel",)),
    )(x2d, g, w, gain, bd)


def _band_unit(q_lo, q_hi, segs, sinks=None):
    res = []
    for h, qh in enumerate((q_lo, q_hi)):
        ss = [_dot_nt(qh, k) + (b_lo, b_hi)[h] for (k, _, b_lo, b_hi) in segs]
        m = jnp.max(functools.reduce(jnp.maximum, ss), axis=-1, keepdims=True)
        if sinks is not None:
            m = jnp.maximum(m, sinks[h])
        ps = [jnp.exp2(s - m) for s in ss]
        l = jnp.sum(functools.reduce(jnp.add, ps), axis=-1, keepdims=True)
        if sinks is not None:
            l = l + jnp.exp2(sinks[h] - m)
        acc = sum(_dot(p.astype(BF16), v) for p, (_, v, _, _) in zip(ps, segs))
        res.append((acc, m, l))
    lo = _lane_lo((BLOCK, LANES))
    (a0, m0, l0), (a1, m1, l1) = res
    return (jnp.where(lo, a0, a1),
            jnp.where(lo, jnp.broadcast_to(m0, (BLOCK, LANES)), jnp.broadcast_to(m1, (BLOCK, LANES))),
            jnp.where(lo, jnp.broadcast_to(l0, (BLOCK, LANES)), jnp.broadcast_to(l1, (BLOCK, LANES))))


def _split_pair(q):
    lo = _lane_lo(q.shape)
    zero = jnp.zeros_like(q)
    return jnp.where(lo, q, zero).astype(BF16), jnp.where(lo, zero, q).astype(BF16)


def _dil_attn_kernel(seq, q0, k0, v0, q1, k1, v1, q2, k2, v2, bias_ref, o_ref,
                     perm_scr, acc_scr, m_scr, l_scr, mrg_scr, nat_scr):
    nblk = seq // BLOCK
    dils = [dil for _, dil in DIL_GROUPS]

    for g, srcs in ((1, (q1, k1, v1)), (2, (q2, k2, v2))):
        dil = dils[g]
        for u in range(nblk):
            r, j = divmod(u, nblk // dil)
            vals = [ref[pl.ds(r + dil * BLOCK * j, BLOCK, stride=dil), :] for ref in srcs]
            for t, val in enumerate(vals):
                perm_scr[(g - 1) * 3 + t, pl.ds(u * BLOCK, BLOCK), :] = val.astype(BF16)

    for g, dil in enumerate(dils):
        if g == 0:
            qr, kr, vr = q0, k0, v0
        else:
            qr, kr, vr = (perm_scr.at[(g - 1) * 3 + t] for t in range(3))
        for u in range(nblk):
            rows = pl.ds(u * BLOCK, BLOCK)
            has_prev = u % (nblk // dil) != 0
            krows = pl.ds((u - 1) * BLOCK, 2 * BLOCK) if has_prev else rows
            lo = 0 if has_prev else BLOCK
            q_lo, q_hi = _split_pair(qr[rows, :].astype(F32))
            acc, m, l = _band_unit(q_lo, q_hi, [(kr[krows, :], vr[krows, :],
                                                 bias_ref[0, g, :, lo:], bias_ref[1, g, :, lo:])])
            acc_scr[g, rows, :] = acc
            m_scr[g, rows, :] = m
            l_scr[g, rows, :] = l

    d_last = dils[-1]
    for r in range(d_last):
        def rows_in(g):
            dil = dils[g]
            start = (r % dil) * (seq // dil) + r // dil
            step = d_last // dil
            return pl.ds(start, BLOCK) if step == 1 else pl.ds(start, BLOCK, stride=step)

        ms = [m_scr[g, rows_in(g), :] for g in range(N_DIL)]
        mm = functools.reduce(jnp.maximum, ms)
        ws = [jnp.exp2(m - mm) for m in ms]
        num = sum(w * acc_scr[g, rows_in(g), :] for g, w in enumerate(ws))
        den = sum(w * l_scr[g, rows_in(g), :] for g, w in enumerate(ws))
        mrg_scr[pl.ds(r * BLOCK, BLOCK), :] = num / den

    for r in range(d_last):
        nat_scr[pl.ds(r, BLOCK, stride=d_last), :] = mrg_scr[pl.ds(r * BLOCK, BLOCK), :]
    o_ref[...] = nat_scr[...].astype(o_ref.dtype)


def _dil_attn(qkv0, qkv12, bias, batch, seq):
    npair = A_HEADS // 2
    nh = npair
    q3 = qkv0.reshape(batch, seq, -1)
    f3 = qkv12.reshape(batch, seq, -1)

    def spec(off):
        return pl.BlockSpec((None, seq, LANES), lambda b, p, off=off: (b, 0, off + p))

    in_specs = ([spec(0), spec(nh), spec(2 * nh)]
                + [spec(0), spec(nh), spec(2 * nh)]
                + [spec(3 * nh), spec(4 * nh), spec(5 * nh)]
                + [pl.BlockSpec((2, N_DIL, BLOCK, 2 * BLOCK), lambda b, p: (p, 0, 0, 0))])
    out = pl.pallas_call(
        functools.partial(_dil_attn_kernel, seq),
        grid=(batch, npair),
        in_specs=in_specs,
        out_specs=pl.BlockSpec((None, seq, LANES), lambda b, p: (b, 0, p)),
        out_shape=jax.ShapeDtypeStruct((batch, seq, A_HEADS * HEAD_DIM), BF16),
        scratch_shapes=([pltpu.VMEM((3 * (N_DIL - 1), seq, LANES), BF16)]
                        + [pltpu.VMEM((N_DIL, seq, LANES), F32)] * 3
                        + [pltpu.VMEM((seq, LANES), F32)] * 2),
        compiler_params=_params(("parallel", "parallel")),
    )(q3, q3, q3, f3, f3, f3, f3, f3, f3, bias)
    return out.reshape(batch * seq, A_HEADS * HEAD_DIM)


def _swa_attn_kernel(seq, sinks_ref, q_ref, k_ref, v_ref, blo_ref, bhi_ref, o_ref):
    p = pl.program_id(1)
    sinks = (sinks_ref[p] * LOG2E, sinks_ref[p + SWA_Q_HEADS // 2] * LOG2E)
    for j in range(seq // BLOCK):
        qrows = pl.ds(j * BLOCK, BLOCK)
        krows = pl.ds((j - 1) * BLOCK, 2 * BLOCK) if j > 0 else qrows
        lo = 0 if j > 0 else BLOCK
        q_lo, q_hi = _split_pair(q_ref[qrows, :].astype(F32))
        acc, _, l = _band_unit(q_lo, q_hi, [(k_ref[krows, :], v_ref[krows, :],
                                             blo_ref[0, :, lo:], bhi_ref[0, :, lo:])], sinks)
        o_ref[qrows, :] = (acc / l).astype(o_ref.dtype)


def _swa_attn(qkv, bias, sinks, batch, seq):
    npair = SWA_Q_HEADS // 2
    q3 = qkv.reshape(batch, seq, -1)
    out = pl.pallas_call(
        functools.partial(_swa_attn_kernel, seq),
        grid=(batch, npair),
        in_specs=[
            pl.BlockSpec(memory_space=pltpu.SMEM),
            pl.BlockSpec((None, seq, LANES), lambda b, p: (b, 0, p)),
            pl.BlockSpec((None, seq, LANES), lambda b, p: (b, 0, npair)),
            pl.BlockSpec((None, seq, LANES), lambda b, p: (b, 0, npair + 1)),
            pl.BlockSpec((None, 1, BLOCK, 2 * BLOCK), lambda b, p: (p, 0, 0, 0)),
            pl.BlockSpec((None, 1, BLOCK, 2 * BLOCK), lambda b, p: (p + npair, 0, 0, 0)),
        ],
        out_specs=pl.BlockSpec((None, seq, LANES), lambda b, p: (b, 0, p)),
        out_shape=jax.ShapeDtypeStruct((batch, seq, SWA_Q_HEADS * HEAD_DIM), BF16),
        compiler_params=_params(("parallel", "parallel")),
    )(sinks, q3, q3, q3, bias, bias)
    return out.reshape(batch * seq, SWA_Q_HEADS * HEAD_DIM)


def _causal_two_maps(seq, q_of, k_of, v_ref, bias_of, vt_scr, s_scr, p_scr, finish):
    for c in range(seq // KV_CHUNK):
        cols = slice(c * KV_CHUNK, (c + 1) * KV_CHUNK)
        vt_scr[:, cols] = v_ref[cols, :].astype(F32).T.astype(BF16)

    units = [(i, mp) for i in range(seq // Q_TILE) for mp in range(2)]
    col_max, results = {}, {}

    def chunks(i):
        return [slice(c * KV_CHUNK, (c + 1) * KV_CHUNK) for c in range(i + 1)]

    def logits_stage(n):
        i, mp = units[n]
        q = q_of(slice(i * Q_TILE, (i + 1) * Q_TILE), mp)
        m = None
        for c, keys in enumerate(chunks(i)):
            s = _dot_nt(k_of(keys, mp), q)
            b = bias_of(i - c, mp)
            if b is not None:
                s = s + b
            s_scr[n % 4, keys, :] = s
            t = jnp.max(s, axis=0, keepdims=True)
            m = t if m is None else jnp.maximum(m, t)
            col_max[n] = m
            yield

    def exp_stage(n):
        i, mp = units[n]
        m, l = col_max[n], None
        for keys in chunks(i):
            p = jnp.exp2(s_scr[n % 4, keys, :] - m)
            t = jnp.sum(p, axis=0, keepdims=True)
            l = t if l is None else l + t
            p_scr[n % 4, keys, :] = p.astype(BF16)
            yield
        width = (i + 1) * KV_CHUNK
        results[n] = (_dot(vt_scr[:, :width], p_scr[n % 4, :width, :]), l)
        if mp == 1:
            finish(slice(i * Q_TILE, (i + 1) * Q_TILE), [results[n - 1], results[n]])

    def emit(*stages):
        live = list(stages)
        while live:
            for s in list(live):
                if next(s, StopIteration) is StopIteration:
                    live.remove(s)

    emit(logits_stage(0))
    for n in range(len(units)):
        emit(*([exp_stage(n)] + ([logits_stage(n + 1)] if n + 1 < len(units) else [])))


def _causal_scratch(seq):
    return [pltpu.VMEM((LANES, seq), BF16), pltpu.VMEM((4, seq, Q_TILE), F32),
            pltpu.VMEM((4, seq, Q_TILE), BF16)]


def _diff_attn_kernel(seq, lam_init, q_ref, k_ref, v_ref, bias0_ref, bias1_ref, lq1, lk1, lq2, lk2,
                      subln_ref, o_ref, vt_scr, s_scr, p_scr):
    lane = lax.broadcasted_iota(jnp.int32, (Q_TILE, LANES), 1)
    lo = (lane < HEAD_DIM).astype(F32).astype(BF16)
    hi = (lane >= HEAD_DIM).astype(F32).astype(BF16)
    lam = (jnp.exp(jnp.sum(lq1[...] * lk1[...], axis=-1, keepdims=True))
           - jnp.exp(jnp.sum(lq2[...] * lk2[...], axis=-1, keepdims=True)) + lam_init)

    def q_of(rows, mp):
        return q_ref[rows, :] * (lo, hi)[mp]

    def k_of(keys, mp):
        return k_ref[keys, :]

    def bias_of(delta, mp):
        return (bias0_ref, bias1_ref)[mp][delta]

    def finish(rows, res):
        (a0, l0), (a1, l1) = res
        ot = a0 * (1.0 / l0) - a1 * (lam / l1)
        ms = jnp.mean(ot * ot, axis=0, keepdims=True)
        ot = ot * lax.rsqrt(ms + RMS_EPS) * (1.0 - lam_init)
        o_ref[rows, :] = (ot.T * subln_ref[...]).astype(o_ref.dtype)

    _causal_two_maps(seq, q_of, k_of, v_ref, bias_of, vt_scr, s_scr, p_scr, finish)


def _diff_attn(qkv, bias, lams, subln, lam_init, batch, seq):
    nh = DIFF_HEADS
    q3 = qkv.reshape(batch, seq, -1)
    ndelta = seq // Q_TILE
    vec = pl.BlockSpec((1, HEAD_DIM), lambda h, b: (0, 0))
    out = pl.pallas_call(
        functools.partial(_diff_attn_kernel, seq, lam_init),
        grid=(nh, batch),
        in_specs=[
            pl.BlockSpec((None, seq, LANES), lambda h, b: (b, 0, h)),
            pl.BlockSpec((None, seq, LANES), lambda h, b: (b, 0, nh + h)),
            pl.BlockSpec((None, seq, LANES), lambda h, b: (b, 0, 2 * nh + h)),
            pl.BlockSpec((None, ndelta, KV_CHUNK, Q_TILE), lambda h, b: (h, 0, 0, 0)),
            pl.BlockSpec((None, ndelta, KV_CHUNK, Q_TILE), lambda h, b: (nh + h, 0, 0, 0)),
            vec, vec, vec, vec,
            pl.BlockSpec((1, LANES), lambda h, b: (0, 0)),
        ],
        out_specs=pl.BlockSpec((None, seq, LANES), lambda h, b: (b, 0, h)),
        out_shape=jax.ShapeDtypeStruct((batch, seq, nh * LANES), BF16),
        scratch_shapes=_causal_scratch(seq),
        compiler_params=_params(("parallel", "parallel")),
    )(q3, q3, q3, bias, bias, *lams, subln)
    return out.reshape(batch * seq, nh * LANES)


def _mla_attn_kernel(seq, q_ref, k_ref, v_ref, mask_ref, o_ref, vt_scr, s_scr, p_scr):
    def q_of(rows, mp):
        return q_ref[rows, mp * LANES:(mp + 1) * LANES]

    def k_of(keys, mp):
        return k_ref[keys, mp * LANES:(mp + 1) * LANES]

    def bias_of(delta, mp):
        return mask_ref[...] if delta == 0 else None

    def finish(rows, res):
        (a0, l0), (a1, l1) = res
        first = lax.broadcasted_iota(jnp.int32, (LANES, Q_TILE), 0) < MLA_V
        ot = jnp.where(first, a0 * (1.0 / l0), a1 * (1.0 / l1))
        o_ref[rows, :] = ot.T.astype(o_ref.dtype)

    _causal_two_maps(seq, q_of, k_of, v_ref, bias_of, vt_scr, s_scr, p_scr, finish)


def _mla_attn(q, k, v, mask, batch, seq):
    npair = MLA_HEADS // 2
    out = pl.pallas_call(
        functools.partial(_mla_attn_kernel, seq),
        grid=(npair, batch),
        in_specs=[
            pl.BlockSpec((None, seq, 2 * LANES), lambda p, b: (b, 0, p)),
            pl.BlockSpec((None, seq, 2 * LANES), lambda p, b: (b, 0, p)),
            pl.BlockSpec((None, seq, LANES), lambda p, b: (b, 0, p)),
            pl.BlockSpec((KV_CHUNK, Q_TILE), lambda p, b: (0, 0)),
        ],
        out_specs=pl.BlockSpec((None, seq, LANES), lambda p, b: (b, 0, p)),
        out_shape=jax.ShapeDtypeStruct((batch, seq, MLA_HEADS * MLA_V), BF16),
        scratch_shapes=_causal_scratch(seq),
        compiler_params=_params(("parallel", "parallel")),
    )(q.reshape(batch, seq, -1), k.reshape(batch, seq, -1), v.reshape(batch, seq, -1), mask)
    return out.reshape(batch * seq, MLA_HEADS * MLA_V)


def _mla_proj_kernel(x_ref, g_ref, w_in_ref, qa_ref, kva_ref, wq_ref, wk_ref, wv_ref, ones_ref,
                     cosq_ref, sinq_ref, cosk_ref, sink_ref, q_out, k_out, v_out):
    hn = _rms(x_ref[...], g_ref[...]).astype(BF16)
    lat = _dot(hn, w_in_ref[...])
    cq = _rms(lat[:, :MLA_Q_RANK], qa_ref[...]).astype(BF16)
    ckv = _rms(lat[:, MLA_Q_RANK:MLA_Q_RANK + MLA_KV_RANK], kva_ref[...]).astype(BF16)
    kpe = lat[:, MLA_Q_RANK + MLA_KV_RANK:]
    kpe2 = jnp.concatenate([kpe, kpe], axis=1)
    v_out[...] = _dot(ckv, wv_ref[...]).astype(v_out.dtype)

    def heads(y, cos_ref, sin_ref):
        ms = _dot((y * y).astype(BF16), ones_ref[...]) * (1.0 / MLA_QK)
        t = y * cos_ref[...] + pltpu.roll(y, 2 * LANES - MLA_ROPE, 1) * sin_ref[...]
        return t * lax.rsqrt(ms + RMS_EPS)

    for hp in range(MLA_HEADS // 2):
        cols = slice(hp * 2 * LANES, (hp + 1) * 2 * LANES)
        q_out[:, cols] = heads(_dot(cq, wq_ref[:, cols]), cosq_ref, sinq_ref).astype(q_out.dtype)
        k_out[:, cols] = heads(_dot(ckv, wk_ref[:, cols]) + kpe2, cosk_ref, sink_ref).astype(k_out.dtype)


def _mla_proj(x2d, g, w_in, qa, kva, wq, wk, wv, ones, tables, seq):
    t = x2d.shape[0]
    per_seq = seq // ROW_TILE
    nq = MLA_HEADS * LANES
    tab = pl.BlockSpec((ROW_TILE, 2 * LANES), lambda i: (i % per_seq, 0))
    return pl.pallas_call(
        _mla_proj_kernel,
        grid=(t // ROW_TILE,),
        in_specs=[
            pl.BlockSpec((ROW_TILE, D_MODEL), lambda i: (i, 0)),
            _const_spec((1, D_MODEL)),
            _const_spec(w_in.shape),
            _const_spec((1, MLA_Q_RANK)),
            _const_spec((1, MLA_KV_RANK)),
            _const_spec(wq.shape),
            _const_spec(wk.shape),
            _const_spec(wv.shape),
            _const_spec((MXU_DIM, MXU_DIM)),
            tab, tab, tab, tab,
        ],
        out_specs=[pl.BlockSpec((ROW_TILE, nq), lambda i: (i, 0)),
                   pl.BlockSpec((ROW_TILE, nq), lambda i: (i, 0)),
                   pl.BlockSpec((ROW_TILE, MLA_HEADS * MLA_V), lambda i: (i, 0))],
        out_shape=[jax.ShapeDtypeStruct((t, nq), BF16), jax.ShapeDtypeStruct((t, nq), BF16),
                   jax.ShapeDtypeStruct((t, MLA_HEADS * MLA_V), BF16)],
        compiler_params=_params(("parallel",)),
    )(x2d, g, w_in, qa, kva, wq, wk, wv, ones, *tables)


def _ffn_kernel(tiles_per_seq, x_ref, o_ref, wo_ref, g_ref, wup_ref, cw_ref, cb_ref, wdn_ref,
                out_ref, h_scr, act_scr, conv_scr, tail_scr):
    rows = x_ref.shape[0]
    x2 = x_ref[...] + _dot(o_ref[...], wo_ref[...])
    out_ref[...] = x2
    h_scr[...] = _rms(x2, g_ref[...]).astype(BF16)
    first = pl.program_id(0) % tiles_per_seq == 0
    for c in range(D_FF // FF_CHUNK):
        cols = slice(c * FF_CHUNK, (c + 1) * FF_CHUNK)
        gate = _dot(h_scr[...], wup_ref[:, cols])
        up = _dot(h_scr[...], wup_ref[:, D_FF + c * FF_CHUNK:D_FF + (c + 1) * FF_CHUNK])
        conv_scr[0:8, :] = jnp.where(first, 0.0, tail_scr[:, cols])
        conv_scr[8:, :] = gate
        tail_scr[:, cols] = gate[rows - 8:, :]
        conv = (cb_ref[:, cols] + cw_ref[2:3, cols] * gate
                + cw_ref[1:2, cols] * conv_scr[7:7 + rows, :]
                + cw_ref[0:1, cols] * conv_scr[6:6 + rows, :])
        act_scr[:, cols] = (conv / (1.0 + jnp.exp(-conv)) * up).astype(BF16)
    out_ref[...] += _dot(act_scr[...], wdn_ref[...])


def _ffn(x2d, o, wo, g, wup, cw, cb, wdn, seq):
    t = x2d.shape[0]
    ko = o.shape[1]
    return pl.pallas_call(
        functools.partial(_ffn_kernel, seq // ROW_TILE),
        grid=(t // ROW_TILE,),
        in_specs=[
            pl.BlockSpec((ROW_TILE, D_MODEL), lambda i: (i, 0)),
            pl.BlockSpec((ROW_TILE, ko), lambda i: (i, 0)),
            _const_spec((ko, D_MODEL)),
            _const_spec((1, D_MODEL)),
            _const_spec((D_MODEL, 2 * D_FF)),
            _const_spec((CONV_WIDTH, D_FF)),
            _const_spec((1, D_FF)),
            _const_spec((D_FF, D_MODEL)),
        ],
        out_specs=pl.BlockSpec((ROW_TILE, D_MODEL), lambda i: (i, 0)),
        out_shape=jax.ShapeDtypeStruct((t, D_MODEL), F32),
        scratch_shapes=[
            pltpu.VMEM((ROW_TILE, D_MODEL), BF16),
            pltpu.VMEM((ROW_TILE, D_FF), BF16),
            pltpu.VMEM((ROW_TILE + 8, FF_CHUNK), F32),
            pltpu.VMEM((8, D_FF), F32),
        ],
        compiler_params=_params(("arbitrary",)),
    )(x2d, o, wo, g, wup, cw, cb, wdn)


def _bias_kernel(tiles, shape, keys_first, tab_ref, out_ref):
    col = pl.program_id(0)
    max_exact = NUM_BUCKETS // 2
    q_axis, k_axis = (1, 0) if keys_first else (0, 1)
    base = (lax.broadcasted_iota(jnp.int32, shape, q_axis)
            - lax.broadcasted_iota(jnp.int32, shape, k_axis))
    for t, (offset, dilation, window) in enumerate(tiles):
        off = base + offset
        dist = jnp.maximum(off, 0) * dilation
        d_f = jnp.maximum(dist, 1).astype(F32)
        large = max_exact + (jnp.log(d_f * (1.0 / max_exact))
                             * ((NUM_BUCKETS - max_exact) / math.log(MAX_DISTANCE / max_exact))
                             ).astype(jnp.int32)
        bucket = jnp.where(dist < max_exact, dist, jnp.minimum(large, NUM_BUCKETS - 1))
        val = jnp.zeros(shape, F32)
        for b in range(NUM_BUCKETS):
            val = jnp.where(bucket == b, tab_ref[b, col], val)
        valid = off >= 0 if window is None else (off >= 0) & (off <= window)
        out_ref[t] = jnp.where(valid, val * LOG2E, NEG)


def _bias_tiles(table, ncols, tiles, shape, keys_first=False):
    return pl.pallas_call(
        functools.partial(_bias_kernel, tiles, shape, keys_first),
        grid=(ncols,),
        in_specs=[pl.BlockSpec(memory_space=pltpu.SMEM)],
        out_specs=pl.BlockSpec((None, len(tiles)) + shape, lambda c: (c, 0, 0, 0)),
        out_shape=jax.ShapeDtypeStruct((ncols, len(tiles)) + shape, F32),
        compiler_params=_params(("parallel",)),
    )(table)


def _causal_mask():
    dist = jnp.arange(Q_TILE)[None, :] - jnp.arange(KV_CHUNK)[:, None]
    return jnp.where(dist >= 0, 0.0, NEG).astype(F32)


def _seg_ones():
    i = jnp.arange(MXU_DIM)
    return ((i[:, None] // HEAD_DIM == i[None, :] // HEAD_DIM).astype(F32) / HEAD_DIM).astype(BF16)


def kernel(x, rel_bias_table, norm_mix, norm_ffn, a_w_in, a_q_norm, a_k_norm, a_w_out, b_w_in, b_q_a_norm, b_kv_a_norm, b_w_q_up, b_w_kv_up, b_q_norm, b_k_norm, b_w_out, c_w_in, c_q_norm, c_k_norm, c_lambda_q1, c_lambda_k1, c_lambda_q2, c_lambda_k2, c_subln, c_w_out, d_w_in, d_q_norm, d_k_norm, d_sinks, d_w_out, f_w_up, f_conv_w, f_conv_b, f_w_down):
    batch, seq, _ = x.shape
    depth = norm_mix.shape[0]
    assert seq == MAX_DISTANCE and seq % ROW_TILE == 0
    table = rel_bias_table.astype(F32)
    bd = _seg_ones()
    qk_scale = HEAD_DIM ** -0.5 * LOG2E
    row = lambda v: v.reshape(1, -1).astype(F32)
    xf = x.reshape(batch * seq, D_MODEL)

    for i in range(depth):
        m, j = i % 4, i // 4
        g_mix = row(norm_mix[i])
        if m == 0:
            hw = A_HEADS * HEAD_DIM
            gain = jnp.concatenate(
                [jnp.concatenate([jnp.tile(a_q_norm[j, g], A_HEADS) * qk_scale,
                                  jnp.tile(a_k_norm[j, g], A_HEADS),
                                  jnp.ones((hw,), F32)]) for g in range(N_DIL)])
            segs = []
            for g in range(N_DIL):
                for t in range(3):
                    for half in range(hw // MXU_DIM):
                        c0 = (g * 3 + t) * hw + half * MXU_DIM
                        oc = c0 if g == 0 else c0 - 3 * hw
                        segs.append((c0, MXU_DIM, t < 2, 0 if g == 0 else 1, oc))
            qkv0, qkv12 = _proj(xf, g_mix, a_w_in[j].astype(BF16), row(gain), bd, tuple(segs),
                                ((3 * hw, BF16), (6 * hw, F32)))
            bias = _bias_tiles(table, A_HEADS,
                               tuple((BLOCK, dil, window // dil) for window, dil in DIL_GROUPS),
                               (BLOCK, 2 * BLOCK))
            o = _dil_attn(qkv0, qkv12, bias, batch, seq)
            w_out = a_w_out[j]
        elif m == 1:
            w_in = b_w_in[j]
            lat_w = MLA_Q_RANK + MLA_KV_RANK
            pe = w_in[:, lat_w:]
            half = MLA_ROPE // 2
            swap = lambda a: jnp.concatenate([a[..., half:], a[..., :half]], axis=-1)
            w_in_ext = jnp.concatenate(
                [w_in[:, :lat_w], jnp.zeros((D_MODEL, MLA_NOPE), F32), pe, swap(pe)], axis=1)
            wq = b_w_q_up[j].reshape(MLA_Q_RANK, MLA_HEADS, MLA_QK)
            wq = jnp.concatenate([wq, swap(wq[..., MLA_NOPE:])], axis=-1).reshape(MLA_Q_RANK, -1)
            wkv = b_w_kv_up[j].reshape(MLA_KV_RANK, MLA_HEADS, MLA_NOPE + MLA_V)
            wk = jnp.concatenate([wkv[..., :MLA_NOPE], jnp.zeros_like(wkv[..., :MLA_NOPE])],
                                 axis=-1).reshape(MLA_KV_RANK, -1)
            wv = wkv[..., MLA_NOPE:].reshape(MLA_KV_RANK, -1)
            inv_freq = ROPE_THETA ** (-jnp.arange(0, MLA_ROPE, 2, dtype=F32) / MLA_ROPE)
            ang = jnp.arange(seq, dtype=F32)[:, None] * inv_freq[None, :]
            cos, sin = jnp.cos(ang), jnp.sin(ang)
            ones = jnp.ones((seq, MLA_NOPE), F32)
            zeros = jnp.zeros((seq, MLA_ROPE), F32)
            cos_t = jnp.concatenate([ones, cos, cos, zeros], axis=1)
            sin_t = jnp.concatenate([0.0 * ones, -sin, sin, zeros], axis=1)

            def rope_tables(gn, scale):
                gpad = jnp.concatenate([gn, swap(gn[MLA_NOPE:])]).astype(F32)
                cos_g = cos_t * gpad[None, :] * scale
                sin_g = sin_t * jnp.roll(gpad, -MLA_ROPE)[None, :] * scale
                return jnp.tile(cos_g, (1, 2)), jnp.tile(sin_g, (1, 2))

            lane = jnp.arange(MXU_DIM)
            real_sum = ((lane[:, None] // LANES == lane[None, :] // LANES)
                        & (lane[:, None] % LANES < MLA_QK)).astype(BF16)
            tables = (rope_tables(b_q_norm[j], MLA_QK ** -0.5 * LOG2E)
                      + rope_tables(b_k_norm[j], 1.0))
            q, k, v = _mla_proj(xf, g_mix, w_in_ext.astype(BF16), row(b_q_a_norm[j]),
                                row(b_kv_a_norm[j]), wq.astype(BF16), wk.astype(BF16),
                                wv.astype(BF16), real_sum, tables, seq)
            o = _mla_attn(q, k, v, _causal_mask(), batch, seq)
            w_out = b_w_out[j]
        elif m == 2:
            qk_w = DIFF_HEADS * 2 * HEAD_DIM
            gain = jnp.concatenate([jnp.tile(c_q_norm[j], 2 * DIFF_HEADS) * qk_scale,
                                    jnp.tile(c_k_norm[j], 2 * DIFF_HEADS), jnp.ones((qk_w,), F32)])
            segs = tuple((c0, MXU_DIM, c0 < 2 * qk_w, 0, c0) for c0 in range(0, 3 * qk_w, MXU_DIM))
            (qkv,) = _proj(xf, g_mix, c_w_in[j].astype(BF16), row(gain), bd, segs,
                           ((3 * qk_w, BF16),))
            bias = _bias_tiles(table, 2 * DIFF_HEADS,
                               tuple((d * Q_TILE, 1, None) for d in range(seq // Q_TILE)),
                               (KV_CHUNK, Q_TILE), keys_first=True)
            lam_init = 0.8 - 0.6 * math.exp(-0.3 * i)
            lams = [row(v) for v in (c_lambda_q1[j], c_lambda_k1[j], c_lambda_q2[j], c_lambda_k2[j])]
            o = _diff_attn(qkv, bias, lams, row(c_subln[j]), lam_init, batch, seq)
            w_out = c_w_out[j]
        else:
            npair = SWA_Q_HEADS // 2
            qw = SWA_Q_HEADS * HEAD_DIM
            order = jnp.arange(SWA_Q_HEADS).reshape(2, npair).T.reshape(-1)
            w_in = d_w_in[j]
            wq = w_in[:, :qw].reshape(D_MODEL, SWA_Q_HEADS, HEAD_DIM)[:, order].reshape(D_MODEL, qw)
            w_perm = jnp.concatenate([wq, w_in[:, qw:]], axis=1)
            kvw = SWA_KV_HEADS * HEAD_DIM
            gain = jnp.concatenate([jnp.tile(d_q_norm[j], SWA_Q_HEADS) * qk_scale,
                                    jnp.tile(d_k_norm[j], SWA_KV_HEADS), jnp.ones((kvw,), F32)])
            segs = tuple((c0, MXU_DIM, True, 0, c0) for c0 in range(0, qw, MXU_DIM))
            segs += ((qw, kvw, True, 0, qw), (qw + kvw, kvw, False, 0, qw + kvw))
            (qkv,) = _proj(xf, g_mix, w_perm.astype(BF16), row(gain), bd, segs,
                           ((qw + 2 * kvw, BF16),))
            bias = _bias_tiles(table, SWA_Q_HEADS, ((BLOCK, 1, SWA_WINDOW - 1),),
                               (BLOCK, 2 * BLOCK))
            o = _swa_attn(qkv, bias, d_sinks[j].astype(F32), batch, seq)
            w_out = d_w_out[j].reshape(SWA_Q_HEADS, HEAD_DIM, D_MODEL)[order].reshape(qw, D_MODEL)
        xf = _ffn(xf, o, w_out.astype(BF16), row(norm_ffn[i]), f_w_up[i].astype(BF16),
                  f_conv_w[i].astype(F32), row(f_conv_b[i]), f_w_down[i].astype(BF16), seq)
    return xf.reshape(batch, seq, D_MODEL)
```

```python
import functools
import math

import jax
import jax.numpy as jnp
from jax import lax
from jax.experimental import pallas as pl
from jax.experimental.pallas import tpu as pltpu

F32 = jnp.float32
BF16 = jnp.bfloat16

D_MODEL = 1024
HEAD_DIM = 64
BLOCK = 128
RMS_EPS = 1e-6
NUM_BUCKETS = 32
MAX_DISTANCE = 2048
DIL_GROUPS = ((128, 1), (512, 4), (2048, 16))
N_DIL = 3
A_HEADS = 8
MLA_HEADS = 16
MLA_NOPE = 64
MLA_ROPE = 32
MLA_QK = 96
MLA_V = 64
MLA_Q_RANK = 384
MLA_KV_RANK = 256
ROPE_THETA = 10000.0
DIFF_HEADS = 8
SWA_Q_HEADS = 16
SWA_KV_HEADS = 2
SWA_WINDOW = 128
D_FF = 2816
CONV_WIDTH = 3

LANES = 128
MXU_DIM = 256
NEG = -1e30
LOG2E = math.log2(math.e)
ROW_TILE = 512
FF_CHUNK = 256
KV_CHUNK = 256
Q_TILE = 256
VMEM_LIMIT = 56 * 1024 * 1024


def _dot(a, b):
    return jnp.dot(a, b, preferred_element_type=F32)


def _dot_nt(a, b):
    return lax.dot_general(a, b, (((1,), (1,)), ((), ())), preferred_element_type=F32)


def _rms(x, g):
    ms = jnp.mean(x * x, axis=-1, keepdims=True)
    return x * lax.rsqrt(ms + RMS_EPS) * g


def _const_spec(shape):
    nd = len(shape)
    return pl.BlockSpec(shape, lambda *_: (0,) * nd, pipeline_mode=pl.Buffered(1))


def _params(sem):
    return pltpu.CompilerParams(dimension_semantics=sem, vmem_limit_bytes=VMEM_LIMIT)


def _lane_lo(shape):
    return lax.broadcasted_iota(jnp.int32, shape, len(shape) - 1) < HEAD_DIM


def _seg_norm(y, bd_ref):
    w = y.shape[-1]
    ms = _dot((y * y).astype(BF16), bd_ref[:w, :w])
    return y * lax.rsqrt(ms + RMS_EPS)


def _proj_kernel(segs, dils, x_ref, g_ref, w_ref, gain_ref, bd_ref, *rest):
    regroup = [d for d in dils if d > 1]
    perm_ref = rest[0] if regroup else None
    out_refs = rest[1:] if regroup else rest
    hn = _rms(x_ref[...], g_ref[...]).astype(BF16)

    def stages(seg):
        c0, width, norm, oi, oc = seg
        y = _dot(hn, w_ref[:, c0:c0 + width])
        yield
        if norm:
            y = _seg_norm(y, bd_ref) * gain_ref[:, c0:c0 + width]
        yb = y.astype(BF16)
        dil = dils[oi]
        if dil == 1:
            out_refs[oi][:, oc:oc + width] = yb
            return
        yield
        yp = _dot(perm_ref[regroup.index(dil)], yb).astype(BF16)
        n = ROW_TILE // dil
        for r in range(dil):
            out_refs[oi][r, :, oc:oc + width] = yp[r * n:(r + 1) * n, :]

    live = []
    for seg in list(segs) + [None, None]:
        if seg is not None:
            live.insert(0, stages(seg))
        for s in list(live):
            if next(s, StopIteration) is StopIteration:
                live.remove(s)


def _regroup_matrix(dil):
    idx = jnp.arange(ROW_TILE)
    n = ROW_TILE // dil
    src = dil * (idx % n) + idx // n
    return (src[:, None] == idx[None, :]).astype(BF16)


def _proj(x2d, g, w, gain, bd, segs, outs, seq):
    t = x2d.shape[0]
    n = w.shape[1]
    per_seq = seq // ROW_TILE
    dils = tuple(d for _, d in outs)
    regroup = [d for d in dils if d > 1]
    in_specs = [
        pl.BlockSpec((ROW_TILE, D_MODEL), lambda i: (i, 0)),
        _const_spec((1, D_MODEL)),
        _const_spec((D_MODEL, n)),
        _const_spec((1, n)),
        _const_spec((MXU_DIM, MXU_DIM)),
    ]
    args = [x2d, g, w, gain, bd]
    if regroup:
        in_specs.append(_const_spec((len(regroup), ROW_TILE, ROW_TILE)))
        args.append(jnp.stack([_regroup_matrix(d) for d in regroup]))
    out_specs, out_shape = [], []
    for ow, d in outs:
        if d == 1:
            out_specs.append(pl.BlockSpec((ROW_TILE, ow), lambda i: (i, 0)))
            out_shape.append(jax.ShapeDtypeStruct((t, ow), BF16))
        else:
            out_specs.append(pl.BlockSpec((None, d, ROW_TILE // d, ow),
                                          lambda i: (i // per_seq, 0, i % per_seq, 0)))
            out_shape.append(jax.ShapeDtypeStruct((t // seq, d, seq // d, ow), BF16))
    return pl.pallas_call(
        functools.partial(_proj_kernel, segs, dils),
        grid=(t // ROW_TILE,),
        in_specs=in_specs,
        out_specs=out_specs,
        out_shape=out_shape,
        compiler_params=_params(("parallel",)),
    )(*args)


def _band_unit(q_lo, q_hi, segs, sinks=None):
    res = []
    for h, qh in enumerate((q_lo, q_hi)):
        ss = [_dot_nt(qh, k) + (b_lo, b_hi)[h] for (k, _, b_lo, b_hi) in segs]
        m = jnp.max(functools.reduce(jnp.maximum, ss), axis=-1, keepdims=True)
        if sinks is not None:
            m = jnp.maximum(m, sinks[h])
        ps = [jnp.exp2(s - m) for s in ss]
        l = jnp.sum(functools.reduce(jnp.add, ps), axis=-1, keepdims=True)
        if sinks is not None:
            l = l + jnp.exp2(sinks[h] - m)
        acc = sum(_dot(p.astype(BF16), v) for p, (_, v, _, _) in zip(ps, segs))
        res.append((acc, m, l))
    lo = _lane_lo((BLOCK, LANES))
    (a0, m0, l0), (a1, m1, l1) = res
    return (jnp.where(lo, a0, a1),
            jnp.where(lo, jnp.broadcast_to(m0, (BLOCK, LANES)), jnp.broadcast_to(m1, (BLOCK, LANES))),
            jnp.where(lo, jnp.broadcast_to(l0, (BLOCK, LANES)), jnp.broadcast_to(l1, (BLOCK, LANES))))


def _split_pair(q):
    lo = _lane_lo(q.shape)
    zero = jnp.zeros_like(q)
    return jnp.where(lo, q, zero).astype(BF16), jnp.where(lo, zero, q).astype(BF16)


def _dil_attn_kernel(seq, q0, k0, v0, q1, k1, v1, q2, k2, v2, bias_ref, o_ref,
                     o_scr, lse_scr, nat_scr):
    nblk = seq // BLOCK
    dils = [dil for _, dil in DIL_GROUPS]
    refs = ((q0, k0, v0), (q1, k1, v1), (q2, k2, v2))

    for g, dil in enumerate(dils):
        qr, kr, vr = refs[g]
        for u in range(nblk):
            rows = pl.ds(u * BLOCK, BLOCK)
            first_key, tile = _band_keys(u, u % (nblk // dil) == 0, nblk)
            krows = pl.ds(first_key, 2 * BLOCK)
            q_lo, q_hi = _split_pair(qr[rows, :].astype(F32))
            acc, m, l = _band_unit(q_lo, q_hi, [(kr[krows, :], vr[krows, :],
                                                 bias_ref[0, N_DIL * tile + g],
                                                 bias_ref[1, N_DIL * tile + g])])
            o_scr[g, rows, :] = acc / l
            lse_scr[g, rows, :] = m + jnp.log2(l)

    for g in range(1, N_DIL):
        for u in range(nblk):
            r, j = divmod(u, nblk // dils[g])
            rows = pl.ds(u * BLOCK, BLOCK)
            positions = pl.ds(r + dils[g] * BLOCK * j, BLOCK, stride=dils[g])
            vals = (o_scr[g, rows, :], lse_scr[g, rows, :])
            for t, val in enumerate(vals):
                nat_scr[2 * (g - 1) + t, positions, :] = val

    for u in range(nblk):
        rows = pl.ds(u * BLOCK, BLOCK)
        outs = [o_scr[0, rows, :]] + [nat_scr[2 * (g - 1), rows, :] for g in range(1, N_DIL)]
        lses = [lse_scr[0, rows, :]] + [nat_scr[2 * (g - 1) + 1, rows, :] for g in range(1, N_DIL)]
        top = functools.reduce(jnp.maximum, lses)
        ws = [jnp.exp2(lse - top) for lse in lses]
        num = sum(w * o for w, o in zip(ws, outs))
        o_ref[rows, :] = (num / sum(ws)).astype(o_ref.dtype)


def _dil_attn(qkvs, bias, batch, seq):
    npair = A_HEADS // 2
    in_specs, args = [], []
    for g in range(N_DIL):
        for t in range(3):
            args.append(qkvs[g].reshape(batch, seq, -1))
            in_specs.append(pl.BlockSpec((None, seq, LANES),
                                         lambda b, p, off=t * npair: (b, 0, off + p)))
    in_specs.append(pl.BlockSpec((2, 3 * N_DIL, BLOCK, 2 * BLOCK), lambda b, p: (p, 0, 0, 0)))
    out = pl.pallas_call(
        functools.partial(_dil_attn_kernel, seq),
        grid=(batch, npair),
        in_specs=in_specs,
        out_specs=pl.BlockSpec((None, seq, LANES), lambda b, p: (b, 0, p)),
        out_shape=jax.ShapeDtypeStruct((batch, seq, A_HEADS * HEAD_DIM), BF16),
        scratch_shapes=([pltpu.VMEM((N_DIL, seq, LANES), F32)] * 2
                        + [pltpu.VMEM((2 * (N_DIL - 1), seq, LANES), F32)]),
        compiler_params=_params(("parallel", "parallel")),
    )(*args, bias)
    return out.reshape(batch * seq, A_HEADS * HEAD_DIM)


def _swa_attn_kernel(seq, sinks_ref, q_ref, k_ref, v_ref, blo_ref, bhi_ref, o_ref):
    p = pl.program_id(1)
    sinks = (sinks_ref[p] * LOG2E, sinks_ref[p + SWA_Q_HEADS // 2] * LOG2E)
    nblk = seq // BLOCK
    for j in range(nblk):
        qrows = pl.ds(j * BLOCK, BLOCK)
        first_key, tile = _band_keys(j, j == 0, nblk)
        krows = pl.ds(first_key, 2 * BLOCK)
        q_lo, q_hi = _split_pair(q_ref[qrows, :].astype(F32))
        acc, _, l = _band_unit(q_lo, q_hi, [(k_ref[krows, :], v_ref[krows, :],
                                             blo_ref[tile], bhi_ref[tile])], sinks)
        o_ref[qrows, :] = (acc / l).astype(o_ref.dtype)


def _swa_attn(qkv, bias, sinks, batch, seq):
    npair = SWA_Q_HEADS // 2
    q3 = qkv.reshape(batch, seq, -1)
    out = pl.pallas_call(
        functools.partial(_swa_attn_kernel, seq),
        grid=(batch, npair),
        in_specs=[
            pl.BlockSpec(memory_space=pltpu.SMEM),
            pl.BlockSpec((None, seq, LANES), lambda b, p: (b, 0, p)),
            pl.BlockSpec((None, seq, LANES), lambda b, p: (b, 0, npair)),
            pl.BlockSpec((None, seq, LANES), lambda b, p: (b, 0, npair + 1)),
            pl.BlockSpec((None, 3, BLOCK, 2 * BLOCK), lambda b, p: (p, 0, 0, 0)),
            pl.BlockSpec((None, 3, BLOCK, 2 * BLOCK), lambda b, p: (p + npair, 0, 0, 0)),
        ],
        out_specs=pl.BlockSpec((None, seq, LANES), lambda b, p: (b, 0, p)),
        out_shape=jax.ShapeDtypeStruct((batch, seq, SWA_Q_HEADS * HEAD_DIM), BF16),
        compiler_params=_params(("parallel", "parallel")),
    )(sinks, q3, q3, q3, bias, bias)
    return out.reshape(batch * seq, SWA_Q_HEADS * HEAD_DIM)


def _causal_two_maps(seq, q_of, k_of, v_ref, bias_of, vt_scr, s_scr, p_scr, finish):
    for c in range(seq // KV_CHUNK):
        cols = slice(c * KV_CHUNK, (c + 1) * KV_CHUNK)
        vt_scr[:, cols] = v_ref[cols, :].astype(F32).T.astype(BF16)

    units = [(i, mp) for i in range(seq // Q_TILE) for mp in range(2)]
    col_max, results = {}, {}

    def chunks(i):
        return [slice(c * KV_CHUNK, (c + 1) * KV_CHUNK) for c in range(i + 1)]

    def logits_stage(n):
        i, mp = units[n]
        q = q_of(slice(i * Q_TILE, (i + 1) * Q_TILE), mp)
        m = None
        for c, keys in enumerate(chunks(i)):
            s = _dot_nt(k_of(keys, mp), q)
            b = bias_of(i - c, mp)
            if b is not None:
                s = s + b
            s_scr[n % 4, keys, :] = s
            t = jnp.max(s, axis=0, keepdims=True)
            m = t if m is None else jnp.maximum(m, t)
            col_max[n] = m
            yield

    def exp_stage(n):
        i, mp = units[n]
        m, l = col_max[n], None
        for keys in chunks(i):
            p = jnp.exp2(s_scr[n % 4, keys, :] - m)
            t = jnp.sum(p, axis=0, keepdims=True)
            l = t if l is None else l + t
            p_scr[n % 4, keys, :] = p.astype(BF16)
            yield
        width = (i + 1) * KV_CHUNK
        results[n] = (_dot(vt_scr[:, :width], p_scr[n % 4, :width, :]), l)
        if mp == 1:
            finish(slice(i * Q_TILE, (i + 1) * Q_TILE), [results[n - 1], results[n]])

    def emit(*stages):
        live = list(stages)
        while live:
            for s in list(live):
                if next(s, StopIteration) is StopIteration:
                    live.remove(s)

    emit(logits_stage(0))
    for n in range(len(units)):
        emit(*([exp_stage(n)] + ([logits_stage(n + 1)] if n + 1 < len(units) else [])))


def _causal_scratch(seq):
    return [pltpu.VMEM((LANES, seq), BF16), pltpu.VMEM((4, seq, Q_TILE), F32),
            pltpu.VMEM((4, seq, Q_TILE), BF16)]


def _diff_attn_kernel(seq, lam_init, q_ref, k_ref, v_ref, bias0_ref, bias1_ref, lq1, lk1, lq2, lk2,
                      subln_ref, o_ref, vt_scr, s_scr, p_scr):
    lane = lax.broadcasted_iota(jnp.int32, (Q_TILE, LANES), 1)
    lo = (lane < HEAD_DIM).astype(F32).astype(BF16)
    hi = (lane >= HEAD_DIM).astype(F32).astype(BF16)
    lam = (jnp.exp(jnp.sum(lq1[...] * lk1[...], axis=-1, keepdims=True))
           - jnp.exp(jnp.sum(lq2[...] * lk2[...], axis=-1, keepdims=True)) + lam_init)

    def q_of(rows, mp):
        return q_ref[rows, :] * (lo, hi)[mp]

    def k_of(keys, mp):
        return k_ref[keys, :]

    def bias_of(delta, mp):
        return (bias0_ref, bias1_ref)[mp][delta]

    def finish(rows, res):
        (a0, l0), (a1, l1) = res
        ot = a0 * (1.0 / l0) - a1 * (lam / l1)
        ms = jnp.mean(ot * ot, axis=0, keepdims=True)
        ot = ot * lax.rsqrt(ms + RMS_EPS) * (1.0 - lam_init)
        o_ref[rows, :] = (ot.T * subln_ref[...]).astype(o_ref.dtype)

    _causal_two_maps(seq, q_of, k_of, v_ref, bias_of, vt_scr, s_scr, p_scr, finish)


def _diff_attn(qkv, bias, lams, subln, lam_init, batch, seq):
    nh = DIFF_HEADS
    q3 = qkv.reshape(batch, seq, -1)
    ndelta = seq // Q_TILE
    vec = pl.BlockSpec((1, HEAD_DIM), lambda h, b: (0, 0))
    out = pl.pallas_call(
        functools.partial(_diff_attn_kernel, seq, lam_init),
        grid=(nh, batch),
        in_specs=[
            pl.BlockSpec((None, seq, LANES), lambda h, b: (b, 0, h)),
            pl.BlockSpec((None, seq, LANES), lambda h, b: (b, 0, nh + h)),
            pl.BlockSpec((None, seq, LANES), lambda h, b: (b, 0, 2 * nh + h)),
            pl.BlockSpec((None, ndelta, KV_CHUNK, Q_TILE), lambda h, b: (h, 0, 0, 0)),
            pl.BlockSpec((None, ndelta, KV_CHUNK, Q_TILE), lambda h, b: (nh + h, 0, 0, 0)),
            vec, vec, vec, vec,
            pl.BlockSpec((1, LANES), lambda h, b: (0, 0)),
        ],
        out_specs=pl.BlockSpec((None, seq, LANES), lambda h, b: (b, 0, h)),
        out_shape=jax.ShapeDtypeStruct((batch, seq, nh * LANES), BF16),
        scratch_shapes=_causal_scratch(seq),
        compiler_params=_params(("parallel", "parallel")),
    )(q3, q3, q3, bias, bias, *lams, subln)
    return out.reshape(batch * seq, nh * LANES)


def _mla_attn_kernel(seq, q_ref, k_ref, v_ref, mask_ref, o_ref, vt_scr, s_scr, p_scr):
    def q_of(rows, mp):
        return q_ref[rows, mp * LANES:(mp + 1) * LANES]

    def k_of(keys, mp):
        return k_ref[keys, mp * LANES:(mp + 1) * LANES]

    def bias_of(delta, mp):
        return mask_ref[...] if delta == 0 else None

    def finish(rows, res):
        (a0, l0), (a1, l1) = res
        first = lax.broadcasted_iota(jnp.int32, (LANES, Q_TILE), 0) < MLA_V
        ot = jnp.where(first, a0 * (1.0 / l0), a1 * (1.0 / l1))
        o_ref[rows, :] = ot.T.astype(o_ref.dtype)

    _causal_two_maps(seq, q_of, k_of, v_ref, bias_of, vt_scr, s_scr, p_scr, finish)


def _mla_attn(q, k, v, mask, batch, seq):
    npair = MLA_HEADS // 2
    out = pl.pallas_call(
        functools.partial(_mla_attn_kernel, seq),
        grid=(npair, batch),
        in_specs=[
            pl.BlockSpec((None, seq, 2 * LANES), lambda p, b: (b, 0, p)),
            pl.BlockSpec((None, seq, 2 * LANES), lambda p, b: (b, 0, p)),
            pl.BlockSpec((None, seq, LANES), lambda p, b: (b, 0, p)),
            pl.BlockSpec((KV_CHUNK, Q_TILE), lambda p, b: (0, 0)),
        ],
        out_specs=pl.BlockSpec((None, seq, LANES), lambda p, b: (b, 0, p)),
        out_shape=jax.ShapeDtypeStruct((batch, seq, MLA_HEADS * MLA_V), BF16),
        scratch_shapes=_causal_scratch(seq),
        compiler_params=_params(("parallel", "parallel")),
    )(q.reshape(batch, seq, -1), k.reshape(batch, seq, -1), v.reshape(batch, seq, -1), mask)
    return out.reshape(batch * seq, MLA_HEADS * MLA_V)


def _mla_proj_kernel(x_ref, g_ref, w_in_ref, qa_ref, kva_ref, wq_ref, wk_ref, wv_ref, ones_ref,
                     cosq_ref, sinq_ref, cosk_ref, sink_ref, q_out, k_out, v_out):
    hn = _rms(x_ref[...], g_ref[...]).astype(BF16)
    lat = _dot(hn, w_in_ref[...])
    cq = _rms(lat[:, :MLA_Q_RANK], qa_ref[...]).astype(BF16)
    ckv = _rms(lat[:, MLA_Q_RANK:MLA_Q_RANK + MLA_KV_RANK], kva_ref[...]).astype(BF16)
    kpe = lat[:, MLA_Q_RANK + MLA_KV_RANK:]
    kpe2 = jnp.concatenate([kpe, kpe], axis=1)
    v_out[...] = _dot(ckv, wv_ref[...]).astype(v_out.dtype)

    def heads(y, cos_ref, sin_ref):
        ms = _dot((y * y).astype(BF16), ones_ref[...]) * (1.0 / MLA_QK)
        t = y * cos_ref[...] + pltpu.roll(y, 2 * LANES - MLA_ROPE, 1) * sin_ref[...]
        return t * lax.rsqrt(ms + RMS_EPS)

    def finish(cols, yq, yk):
        q_out[:, cols] = heads(yq, cosq_ref, sinq_ref).astype(q_out.dtype)
        k_out[:, cols] = heads(yk + kpe2, cosk_ref, sink_ref).astype(k_out.dtype)

    pending = None
    for hp in range(MLA_HEADS // 2):
        cols = slice(hp * 2 * LANES, (hp + 1) * 2 * LANES)
        yq = _dot(cq, wq_ref[:, cols])
        yk = _dot(ckv, wk_ref[:, cols])
        if pending is not None:
            finish(*pending)
        pending = (cols, yq, yk)
    finish(*pending)


def _mla_proj(x2d, g, w_in, qa, kva, wq, wk, wv, ones, tables, seq):
    t = x2d.shape[0]
    per_seq = seq // ROW_TILE
    nq = MLA_HEADS * LANES
    tab = pl.BlockSpec((ROW_TILE, 2 * LANES), lambda i: (i % per_seq, 0))
    return pl.pallas_call(
        _mla_proj_kernel,
        grid=(t // ROW_TILE,),
        in_specs=[
            pl.BlockSpec((ROW_TILE, D_MODEL), lambda i: (i, 0)),
            _const_spec((1, D_MODEL)),
            _const_spec(w_in.shape),
            _const_spec((1, MLA_Q_RANK)),
            _const_spec((1, MLA_KV_RANK)),
            _const_spec(wq.shape),
            _const_spec(wk.shape),
            _const_spec(wv.shape),
            _const_spec((MXU_DIM, MXU_DIM)),
            tab, tab, tab, tab,
        ],
        out_specs=[pl.BlockSpec((ROW_TILE, nq), lambda i: (i, 0)),
                   pl.BlockSpec((ROW_TILE, nq), lambda i: (i, 0)),
                   pl.BlockSpec((ROW_TILE, MLA_HEADS * MLA_V), lambda i: (i, 0))],
        out_shape=[jax.ShapeDtypeStruct((t, nq), BF16), jax.ShapeDtypeStruct((t, nq), BF16),
                   jax.ShapeDtypeStruct((t, MLA_HEADS * MLA_V), BF16)],
        compiler_params=_params(("parallel",)),
    )(x2d, g, w_in, qa, kva, wq, wk, wv, ones, *tables)


def _ffn_kernel(tiles_per_seq, x_ref, o_ref, wo_ref, g_ref, wup_ref, cw_ref, cb_ref, wdn_ref,
                out_ref, h_scr, act_scr, conv_scr, tail_scr):
    rows = x_ref.shape[0]
    x2 = x_ref[...] + _dot(o_ref[...], wo_ref[...])
    out_ref[...] = x2
    h_scr[...] = _rms(x2, g_ref[...]).astype(BF16)
    first = pl.program_id(0) % tiles_per_seq == 0
    for c in range(D_FF // FF_CHUNK):
        cols = slice(c * FF_CHUNK, (c + 1) * FF_CHUNK)
        gate = _dot(h_scr[...], wup_ref[:, cols])
        up = _dot(h_scr[...], wup_ref[:, D_FF + c * FF_CHUNK:D_FF + (c + 1) * FF_CHUNK])
        conv_scr[0:8, :] = jnp.where(first, 0.0, tail_scr[:, cols])
        conv_scr[8:, :] = gate
        tail_scr[:, cols] = gate[rows - 8:, :]
        conv = (cb_ref[:, cols] + cw_ref[2:3, cols] * gate
                + cw_ref[1:2, cols] * conv_scr[7:7 + rows, :]
                + cw_ref[0:1, cols] * conv_scr[6:6 + rows, :])
        act_scr[:, cols] = (conv / (1.0 + jnp.exp(-conv)) * up).astype(BF16)
    out_ref[...] += _dot(act_scr[...], wdn_ref[...])


def _ffn(x2d, o, wo, g, wup, cw, cb, wdn, seq):
    t = x2d.shape[0]
    ko = o.shape[1]
    return pl.pallas_call(
        functools.partial(_ffn_kernel, seq // ROW_TILE),
        grid=(t // ROW_TILE,),
        in_specs=[
            pl.BlockSpec((ROW_TILE, D_MODEL), lambda i: (i, 0)),
            pl.BlockSpec((ROW_TILE, ko), lambda i: (i, 0)),
            _const_spec((ko, D_MODEL)),
            _const_spec((1, D_MODEL)),
            _const_spec((D_MODEL, 2 * D_FF)),
            _const_spec((CONV_WIDTH, D_FF)),
            _const_spec((1, D_FF)),
            _const_spec((D_FF, D_MODEL)),
        ],
        out_specs=pl.BlockSpec((ROW_TILE, D_MODEL), lambda i: (i, 0)),
        out_shape=jax.ShapeDtypeStruct((t, D_MODEL), F32),
        scratch_shapes=[
            pltpu.VMEM((ROW_TILE, D_MODEL), BF16),
            pltpu.VMEM((ROW_TILE, D_FF), BF16),
            pltpu.VMEM((ROW_TILE + 8, FF_CHUNK), F32),
            pltpu.VMEM((8, D_FF), F32),
        ],
        compiler_params=_params(("arbitrary",)),
    )(x2d, o, wo, g, wup, cw, cb, wdn)


def _bias_kernel(tiles, shape, keys_first, tab_ref, out_ref):
    col = pl.program_id(0)
    max_exact = NUM_BUCKETS // 2
    q_axis, k_axis = (1, 0) if keys_first else (0, 1)
    base = (lax.broadcasted_iota(jnp.int32, shape, q_axis)
            - lax.broadcasted_iota(jnp.int32, shape, k_axis))
    for t, (offset, dilation, window, first_key) in enumerate(tiles):
        off = base + offset
        dist = jnp.maximum(off, 0) * dilation
        d_f = jnp.maximum(dist, 1).astype(F32)
        large = max_exact + (jnp.log(d_f * (1.0 / max_exact))
                             * ((NUM_BUCKETS - max_exact) / math.log(MAX_DISTANCE / max_exact))
                             ).astype(jnp.int32)
        bucket = jnp.where(dist < max_exact, dist, jnp.minimum(large, NUM_BUCKETS - 1))
        val = jnp.zeros(shape, F32)
        for b in range(NUM_BUCKETS):
            val = jnp.where(bucket == b, tab_ref[b, col], val)
        val = jnp.where(off >= 0, val * LOG2E, NEG)
        if window is not None:
            val = jnp.where(off <= window, val, NEG)
        if first_key:
            val = jnp.where(lax.broadcasted_iota(jnp.int32, shape, k_axis) >= first_key, val, NEG)
        out_ref[t] = val


def _bias_tiles(table, ncols, tiles, shape, keys_first=False):
    return pl.pallas_call(
        functools.partial(_bias_kernel, tiles, shape, keys_first),
        grid=(ncols,),
        in_specs=[pl.BlockSpec(memory_space=pltpu.SMEM)],
        out_specs=pl.BlockSpec((None, len(tiles)) + shape, lambda c: (c, 0, 0, 0)),
        out_shape=jax.ShapeDtypeStruct((ncols, len(tiles)) + shape, F32),
        compiler_params=_params(("parallel",)),
    )(table)


def _band_tiles(dilation, window):
    return ((BLOCK, dilation, window, 0), (0, dilation, window, 0),
            (BLOCK, dilation, window, BLOCK))


BAND_PREV, BAND_NEXT, BAND_LAST = range(3)


def _band_keys(u, starts_sequence, nblk):
    if not starts_sequence:
        return (u - 1) * BLOCK, BAND_PREV
    if u + 1 < nblk:
        return u * BLOCK, BAND_NEXT
    return (u - 1) * BLOCK, BAND_LAST


def _causal_mask():
    dist = jnp.arange(Q_TILE)[None, :] - jnp.arange(KV_CHUNK)[:, None]
    return jnp.where(dist >= 0, 0.0, NEG).astype(F32)


def _seg_ones():
    i = jnp.arange(MXU_DIM)
    return ((i[:, None] // HEAD_DIM == i[None, :] // HEAD_DIM).astype(F32) / HEAD_DIM).astype(BF16)


def kernel(x, rel_bias_table, norm_mix, norm_ffn, a_w_in, a_q_norm, a_k_norm, a_w_out, b_w_in, b_q_a_norm, b_kv_a_norm, b_w_q_up, b_w_kv_up, b_q_norm, b_k_norm, b_w_out, c_w_in, c_q_norm, c_k_norm, c_lambda_q1, c_lambda_k1, c_lambda_q2, c_lambda_k2, c_subln, c_w_out, d_w_in, d_q_norm, d_k_norm, d_sinks, d_w_out, f_w_up, f_conv_w, f_conv_b, f_w_down):
    batch, seq, _ = x.shape
    depth = norm_mix.shape[0]
    assert seq == MAX_DISTANCE and seq % ROW_TILE == 0
    table = rel_bias_table.astype(F32)
    bd = _seg_ones()
    qk_scale = HEAD_DIM ** -0.5 * LOG2E
    row = lambda v: v.reshape(1, -1).astype(F32)
    xf = x.reshape(batch * seq, D_MODEL)

    for i in range(depth):
        m, j = i % 4, i // 4
        g_mix = row(norm_mix[i])
        if m == 0:
            hw = A_HEADS * HEAD_DIM
            gain = jnp.concatenate(
                [jnp.concatenate([jnp.tile(a_q_norm[j, g], A_HEADS) * qk_scale,
                                  jnp.tile(a_k_norm[j, g], A_HEADS),
                                  jnp.ones((hw,), F32)]) for g in range(N_DIL)])
            segs = []
            for g in range(N_DIL):
                for t in range(3):
                    for half in range(hw // MXU_DIM):
                        c0 = (g * 3 + t) * hw + half * MXU_DIM
                        segs.append((c0, MXU_DIM, t < 2, g, c0 - g * 3 * hw))
            qkvs = _proj(xf, g_mix, a_w_in[j].astype(BF16), row(gain), bd, tuple(segs),
                         tuple((3 * hw, dil) for _, dil in DIL_GROUPS), seq)
            bias = _bias_tiles(table, A_HEADS,
                               tuple(_band_tiles(dil, window // dil)[tile] for tile in range(3)
                                     for window, dil in DIL_GROUPS),
                               (BLOCK, 2 * BLOCK))
            o = _dil_attn(qkvs, bias, batch, seq)
            w_out = a_w_out[j]
        elif m == 1:
            w_in = b_w_in[j]
            lat_w = MLA_Q_RANK + MLA_KV_RANK
            pe = w_in[:, lat_w:]
            half = MLA_ROPE // 2
            swap = lambda a: jnp.concatenate([a[..., half:], a[..., :half]], axis=-1)
            w_in_ext = jnp.concatenate(
                [w_in[:, :lat_w], jnp.zeros((D_MODEL, MLA_NOPE), F32), pe, swap(pe)], axis=1)
            wq = b_w_q_up[j].reshape(MLA_Q_RANK, MLA_HEADS, MLA_QK)
            wq = jnp.concatenate([wq, swap(wq[..., MLA_NOPE:])], axis=-1).reshape(MLA_Q_RANK, -1)
            wkv = b_w_kv_up[j].reshape(MLA_KV_RANK, MLA_HEADS, MLA_NOPE + MLA_V)
            wk = jnp.concatenate([wkv[..., :MLA_NOPE], jnp.zeros_like(wkv[..., :MLA_NOPE])],
                                 axis=-1).reshape(MLA_KV_RANK, -1)
            wv = wkv[..., MLA_NOPE:].reshape(MLA_KV_RANK, -1)
            inv_freq = ROPE_THETA ** (-jnp.arange(0, MLA_ROPE, 2, dtype=F32) / MLA_ROPE)
            ang = jnp.arange(seq, dtype=F32)[:, None] * inv_freq[None, :]
            cos, sin = jnp.cos(ang), jnp.sin(ang)
            ones = jnp.ones((seq, MLA_NOPE), F32)
            zeros = jnp.zeros((seq, MLA_ROPE), F32)
            cos_t = jnp.concatenate([ones, cos, cos, zeros], axis=1)
            sin_t = jnp.concatenate([0.0 * ones, -sin, sin, zeros], axis=1)

            def rope_tables(gn, scale):
                gpad = jnp.concatenate([gn, swap(gn[MLA_NOPE:])]).astype(F32)
                cos_g = cos_t * gpad[None, :] * scale
                sin_g = sin_t * jnp.roll(gpad, -MLA_ROPE)[None, :] * scale
                return jnp.tile(cos_g, (1, 2)), jnp.tile(sin_g, (1, 2))

            lane = jnp.arange(MXU_DIM)
            real_sum = ((lane[:, None] // LANES == lane[None, :] // LANES)
                        & (lane[:, None] % LANES < MLA_QK)).astype(BF16)
            tables = (rope_tables(b_q_norm[j], MLA_QK ** -0.5 * LOG2E)
                      + rope_tables(b_k_norm[j], 1.0))
            q, k, v = _mla_proj(xf, g_mix, w_in_ext.astype(BF16), row(b_q_a_norm[j]),
                                row(b_kv_a_norm[j]), wq.astype(BF16), wk.astype(BF16),
                                wv.astype(BF16), real_sum, tables, seq)
            o = _mla_attn(q, k, v, _causal_mask(), batch, seq)
            w_out = b_w_out[j]
        elif m == 2:
            qk_w = DIFF_HEADS * 2 * HEAD_DIM
            gain = jnp.concatenate([jnp.tile(c_q_norm[j], 2 * DIFF_HEADS) * qk_scale,
                                    jnp.tile(c_k_norm[j], 2 * DIFF_HEADS), jnp.ones((qk_w,), F32)])
            segs = tuple((c0, MXU_DIM, c0 < 2 * qk_w, 0, c0) for c0 in range(0, 3 * qk_w, MXU_DIM))
            (qkv,) = _proj(xf, g_mix, c_w_in[j].astype(BF16), row(gain), bd, segs,
                           ((3 * qk_w, 1),), seq)
            bias = _bias_tiles(table, 2 * DIFF_HEADS,
                               tuple((d * Q_TILE, 1, None, 0) for d in range(seq // Q_TILE)),
                               (KV_CHUNK, Q_TILE), keys_first=True)
            lam_init = 0.8 - 0.6 * math.exp(-0.3 * i)
            lams = [row(v) for v in (c_lambda_q1[j], c_lambda_k1[j], c_lambda_q2[j], c_lambda_k2[j])]
            o = _diff_attn(qkv, bias, lams, row(c_subln[j]), lam_init, batch, seq)
            w_out = c_w_out[j]
        else:
            npair = SWA_Q_HEADS // 2
            qw = SWA_Q_HEADS * HEAD_DIM
            order = jnp.arange(SWA_Q_HEADS).reshape(2, npair).T.reshape(-1)
            w_in = d_w_in[j]
            wq = w_in[:, :qw].reshape(D_MODEL, SWA_Q_HEADS, HEAD_DIM)[:, order].reshape(D_MODEL, qw)
            w_perm = jnp.concatenate([wq, w_in[:, qw:]], axis=1)
            kvw = SWA_KV_HEADS * HEAD_DIM
            gain = jnp.concatenate([jnp.tile(d_q_norm[j], SWA_Q_HEADS) * qk_scale,
                                    jnp.tile(d_k_norm[j], SWA_KV_HEADS), jnp.ones((kvw,), F32)])
            segs = tuple((c0, MXU_DIM, True, 0, c0) for c0 in range(0, qw, MXU_DIM))
            segs += ((qw, kvw, True, 0, qw), (qw + kvw, kvw, False, 0, qw + kvw))
            (qkv,) = _proj(xf, g_mix, w_perm.astype(BF16), row(gain), bd, segs,
                           ((qw + 2 * kvw, 1),), seq)
            bias = _bias_tiles(table, SWA_Q_HEADS, _band_tiles(1, SWA_WINDOW - 1),
                               (BLOCK, 2 * BLOCK))
            o = _swa_attn(qkv, bias, d_sinks[j].astype(F32), batch, seq)
            w_out = d_w_out[j].reshape(SWA_Q_HEADS, HEAD_DIM, D_MODEL)[order].reshape(qw, D_MODEL)
        xf = _ffn(xf, o, w_out.astype(BF16), row(norm_ffn[i]), f_w_up[i].astype(BF16),
                  f_conv_w[i].astype(F32), row(f_conv_b[i]), f_w_down[i].astype(BF16), seq)
    return xf.reshape(batch, seq, D_MODEL)
```

```python
import functools
import math

import jax
import jax.numpy as jnp
from jax import lax
from jax.experimental import pallas as pl
from jax.experimental.pallas import tpu as pltpu

F32 = jnp.float32
BF16 = jnp.bfloat16

D_MODEL = 1024
HEAD_DIM = 64
BLOCK = 128
RMS_EPS = 1e-6
NUM_BUCKETS = 32
MAX_DISTANCE = 2048
DIL_GROUPS = ((128, 1), (512, 4), (2048, 16))
N_DIL = 3
A_HEADS = 8
MLA_HEADS = 16
MLA_NOPE = 64
MLA_ROPE = 32
MLA_QK = 96
MLA_V = 64
MLA_Q_RANK = 384
MLA_KV_RANK = 256
ROPE_THETA = 10000.0
DIFF_HEADS = 8
SWA_Q_HEADS = 16
SWA_KV_HEADS = 2
SWA_WINDOW = 128
D_FF = 2816
CONV_WIDTH = 3

LANES = 128
SUBLANES = 8
MXU_DIM = 256
NEG = -1e30
LOG2E = math.log2(math.e)
ROW_TILE = 512
FF_CHUNK = 256
KV_CHUNK = 256
Q_TILE = 256
STAGE_SLOTS = 4
VMEM_LIMIT = 56 * 1024 * 1024


def _dot(a, b):
    return jnp.dot(a, b, preferred_element_type=F32)


def _dot_nt(a, b):
    return lax.dot_general(a, b, (((1,), (1,)), ((), ())), preferred_element_type=F32)


def _rms(x, g):
    ms = jnp.mean(x * x, axis=-1, keepdims=True)
    return x * lax.rsqrt(ms + RMS_EPS) * g


def _const_spec(shape):
    nd = len(shape)
    return pl.BlockSpec(shape, lambda *_: (0,) * nd, pipeline_mode=pl.Buffered(1))


def _params(sem):
    return pltpu.CompilerParams(dimension_semantics=sem, vmem_limit_bytes=VMEM_LIMIT)


def _lane_lo(shape):
    return lax.broadcasted_iota(jnp.int32, shape, len(shape) - 1) < HEAD_DIM


def _seg_norm(y, bd_ref):
    w = y.shape[-1]
    ms = _dot((y * y).astype(BF16), bd_ref[:w, :w])
    return y * lax.rsqrt(ms + RMS_EPS)


def _proj_kernel(segs, dils, x_ref, g_ref, w_ref, gain_ref, bd_ref, *rest):
    regroup = [d for d in dils if d > 1]
    perm_ref = rest[0] if regroup else None
    out_refs = rest[1:] if regroup else rest
    hn = _rms(x_ref[...], g_ref[...]).astype(BF16)

    def stages(seg):
        c0, width, norm_cols, oi, oc = seg
        y = _dot(hn, w_ref[:, c0:c0 + width])
        yield
        if norm_cols:
            yn = _seg_norm(y, bd_ref) * gain_ref[:, c0:c0 + width]
            lane = lax.broadcasted_iota(jnp.int32, y.shape, 1)
            y = yn if norm_cols == width else jnp.where(lane < norm_cols, yn, y)
        yb = y.astype(BF16)
        dil = dils[oi]
        if dil == 1:
            out_refs[oi][:, oc:oc + width] = yb
            return
        yield
        yp = _dot(perm_ref[regroup.index(dil)], yb).astype(BF16)
        n = ROW_TILE // dil
        for r in range(dil):
            out_refs[oi][r, :, oc:oc + width] = yp[r * n:(r + 1) * n, :]

    live = []
    for seg in list(segs) + [None, None]:
        if seg is not None:
            live.insert(0, stages(seg))
        for s in list(live):
            if next(s, StopIteration) is StopIteration:
                live.remove(s)


def _regroup_matrix(dil):
    idx = jnp.arange(ROW_TILE)
    n = ROW_TILE // dil
    src = dil * (idx % n) + idx // n
    return (src[:, None] == idx[None, :]).astype(BF16)


def _proj(x2d, g, w, gain, bd, segs, outs, seq):
    t = x2d.shape[0]
    n = w.shape[1]
    per_seq = seq // ROW_TILE
    dils = tuple(d for _, d in outs)
    regroup = [d for d in dils if d > 1]
    in_specs = [
        pl.BlockSpec((ROW_TILE, D_MODEL), lambda i: (i, 0)),
        _const_spec((1, D_MODEL)),
        _const_spec((D_MODEL, n)),
        _const_spec((1, n)),
        _const_spec((MXU_DIM, MXU_DIM)),
    ]
    args = [x2d, g, w, gain, bd]
    if regroup:
        in_specs.append(_const_spec((len(regroup), ROW_TILE, ROW_TILE)))
        args.append(jnp.stack([_regroup_matrix(d) for d in regroup]))
    out_specs, out_shape = [], []
    for ow, d in outs:
        if d == 1:
            out_specs.append(pl.BlockSpec((ROW_TILE, ow), lambda i: (i, 0)))
            out_shape.append(jax.ShapeDtypeStruct((t, ow), BF16))
        else:
            out_specs.append(pl.BlockSpec((None, d, ROW_TILE // d, ow),
                                          lambda i: (i // per_seq, 0, i % per_seq, 0)))
            out_shape.append(jax.ShapeDtypeStruct((t // seq, d, seq // d, ow), BF16))
    return pl.pallas_call(
        functools.partial(_proj_kernel, segs, dils),
        grid=(t // ROW_TILE,),
        in_specs=in_specs,
        out_specs=out_specs,
        out_shape=out_shape,
        compiler_params=_params(("parallel",)),
    )(*args)


def _band_unit(q_pair, k, v, biases, sinks=None):
    res = []
    for h, qh in enumerate(_split_pair(q_pair)):
        s = _dot_nt(qh, k) + biases[h]
        m = jnp.max(s, axis=-1, keepdims=True)
        if sinks is not None:
            m = jnp.maximum(m, sinks[h])
        p = jnp.exp2(s - m)
        l = jnp.sum(p, axis=-1, keepdims=True)
        if sinks is not None:
            l = l + jnp.exp2(sinks[h] - m)
        res.append((_dot(p.astype(BF16), v), m, l))
    lo = _lane_lo((BLOCK, LANES))
    (a0, m0, l0), (a1, m1, l1) = res
    return (jnp.where(lo, a0, a1),
            jnp.where(lo, jnp.broadcast_to(m0, (BLOCK, LANES)), jnp.broadcast_to(m1, (BLOCK, LANES))),
            jnp.where(lo, jnp.broadcast_to(l0, (BLOCK, LANES)), jnp.broadcast_to(l1, (BLOCK, LANES))))


def _split_pair(q):
    q = q.astype(F32)
    lo = _lane_lo(q.shape)
    zero = jnp.zeros_like(q)
    return jnp.where(lo, q, zero).astype(BF16), jnp.where(lo, zero, q).astype(BF16)


def _dil_attn_kernel(seq, q0, k0, v0, q1, k1, v1, q2, k2, v2, bias_ref, o_ref,
                     o_scr, lse_scr, nat_scr):
    nblk = seq // BLOCK
    dils = [dil for _, dil in DIL_GROUPS]
    refs = ((q0, k0, v0), (q1, k1, v1), (q2, k2, v2))

    for g, dil in enumerate(dils):
        qr, kr, vr = refs[g]
        for u in range(nblk):
            rows = pl.ds(u * BLOCK, BLOCK)
            first_key, tile = _band_keys(u, u % (nblk // dil) == 0, nblk)
            krows = pl.ds(first_key, 2 * BLOCK)
            acc, m, l = _band_unit(qr[rows, :], kr[krows, :], vr[krows, :],
                                   [bias_ref[h, N_DIL * tile + g] for h in range(2)])
            o_scr[g, rows, :] = acc / l
            lse_scr[g, rows, :] = m + jnp.log2(l)

    for g in range(1, N_DIL):
        for u in range(nblk):
            r, j = divmod(u, nblk // dils[g])
            rows = pl.ds(u * BLOCK, BLOCK)
            positions = pl.ds(r + dils[g] * BLOCK * j, BLOCK, stride=dils[g])
            vals = (o_scr[g, rows, :], lse_scr[g, rows, :])
            for t, val in enumerate(vals):
                nat_scr[2 * (g - 1) + t, positions, :] = val

    for u in range(nblk):
        rows = pl.ds(u * BLOCK, BLOCK)
        outs = [o_scr[0, rows, :]] + [nat_scr[2 * (g - 1), rows, :] for g in range(1, N_DIL)]
        lses = [lse_scr[0, rows, :]] + [nat_scr[2 * (g - 1) + 1, rows, :] for g in range(1, N_DIL)]
        top = functools.reduce(jnp.maximum, lses)
        ws = [jnp.exp2(lse - top) for lse in lses]
        num = sum(w * o for w, o in zip(ws, outs))
        o_ref[rows, :] = (num / sum(ws)).astype(o_ref.dtype)


def _dil_attn(qkvs, bias, batch, seq):
    npair = A_HEADS // 2
    in_specs, args = [], []
    for g in range(N_DIL):
        for t in range(3):
            args.append(qkvs[g].reshape(batch, seq, -1))
            in_specs.append(pl.BlockSpec((None, seq, LANES),
                                         lambda b, p, off=t * npair: (b, 0, off + p)))
    in_specs.append(pl.BlockSpec((2, 3 * N_DIL, BLOCK, 2 * BLOCK), lambda b, p: (p, 0, 0, 0)))
    out = pl.pallas_call(
        functools.partial(_dil_attn_kernel, seq),
        grid=(batch, npair),
        in_specs=in_specs,
        out_specs=pl.BlockSpec((None, seq, LANES), lambda b, p: (b, 0, p)),
        out_shape=jax.ShapeDtypeStruct((batch, seq, A_HEADS * HEAD_DIM), BF16),
        scratch_shapes=([pltpu.VMEM((N_DIL, seq, LANES), F32)] * 2
                        + [pltpu.VMEM((2 * (N_DIL - 1), seq, LANES), F32)]),
        compiler_params=_params(("parallel", "parallel")),
    )(*args, bias)
    return out.reshape(batch * seq, A_HEADS * HEAD_DIM)


def _swa_attn_kernel(seq, sinks_ref, q_ref, k_ref, v_ref, blo_ref, bhi_ref, o_ref):
    p = pl.program_id(1)
    sinks = (sinks_ref[p] * LOG2E, sinks_ref[p + SWA_Q_HEADS // 2] * LOG2E)
    nblk = seq // BLOCK
    for j in range(nblk):
        qrows = pl.ds(j * BLOCK, BLOCK)
        first_key, tile = _band_keys(j, j == 0, nblk)
        krows = pl.ds(first_key, 2 * BLOCK)
        acc, _, l = _band_unit(q_ref[qrows, :], k_ref[krows, :], v_ref[krows, :],
                               (blo_ref[tile], bhi_ref[tile]), sinks)
        o_ref[qrows, :] = (acc / l).astype(o_ref.dtype)


def _swa_attn(qkv, bias, sinks, batch, seq):
    npair = SWA_Q_HEADS // 2
    q3 = qkv.reshape(batch, seq, -1)
    out = pl.pallas_call(
        functools.partial(_swa_attn_kernel, seq),
        grid=(batch, npair),
        in_specs=[
            pl.BlockSpec(memory_space=pltpu.SMEM),
            pl.BlockSpec((None, seq, LANES), lambda b, p: (b, 0, p)),
            pl.BlockSpec((None, seq, LANES), lambda b, p: (b, 0, npair)),
            pl.BlockSpec((None, seq, LANES), lambda b, p: (b, 0, npair + 1)),
            pl.BlockSpec((None, 3, BLOCK, 2 * BLOCK), lambda b, p: (p, 0, 0, 0)),
            pl.BlockSpec((None, 3, BLOCK, 2 * BLOCK), lambda b, p: (p + npair, 0, 0, 0)),
        ],
        out_specs=pl.BlockSpec((None, seq, LANES), lambda b, p: (b, 0, p)),
        out_shape=jax.ShapeDtypeStruct((batch, seq, SWA_Q_HEADS * HEAD_DIM), BF16),
        compiler_params=_params(("parallel", "parallel")),
    )(sinks, q3, q3, q3, bias, bias)
    return out.reshape(batch * seq, SWA_Q_HEADS * HEAD_DIM)


def _causal_two_maps(seq, q_of, k_of, v_ref, bias_of, vt_scr, s_scr, p_scr, finish):
    for c in range(seq // KV_CHUNK):
        cols = slice(c * KV_CHUNK, (c + 1) * KV_CHUNK)
        vt_scr[:, cols] = v_ref[cols, :].astype(F32).T.astype(BF16)

    units = [(i, mp) for i in range(seq // Q_TILE) for mp in range(2)]
    col_max, results = {}, {}

    def chunks(i):
        return [slice(c * KV_CHUNK, (c + 1) * KV_CHUNK) for c in range(i + 1)]

    def logits_stage(n):
        i, mp = units[n]
        q = q_of(slice(i * Q_TILE, (i + 1) * Q_TILE), mp)
        m = None
        for c, keys in enumerate(chunks(i)):
            s = _dot_nt(k_of(keys, mp), q)
            b = bias_of(i - c, mp)
            if b is not None:
                s = s + b
            s_scr[n % STAGE_SLOTS, keys, :] = s
            for r0 in range(0, KV_CHUNK, SUBLANES):
                t = s[r0:r0 + SUBLANES, :]
                m = t if m is None else jnp.maximum(m, t)
            col_max[n] = m
            yield

    def exp_stage(n):
        i, mp = units[n]
        slot = n % STAGE_SLOTS
        m, l = jnp.max(col_max[n], axis=0, keepdims=True), None
        for keys in chunks(i):
            p = jnp.exp2(s_scr[slot, keys, :] - m)
            for r0 in range(0, KV_CHUNK, SUBLANES):
                t = p[r0:r0 + SUBLANES, :]
                l = t if l is None else l + t
            p_scr[slot, keys, :] = p.astype(BF16)
            yield
        l = jnp.sum(l, axis=0, keepdims=True)
        width = (i + 1) * KV_CHUNK
        results[n] = (_dot(vt_scr[:, :width], p_scr[slot, :width, :]), l)
        if mp == 1:
            finish(slice(i * Q_TILE, (i + 1) * Q_TILE), [results[n - 1], results[n]])

    def emit(*stages):
        live = list(stages)
        while live:
            for s in list(live):
                if next(s, StopIteration) is StopIteration:
                    live.remove(s)

    emit(logits_stage(0))
    for n in range(len(units)):
        emit(*(([logits_stage(n + 1)] if n + 1 < len(units) else []) + [exp_stage(n)]))


def _causal_scratch(seq):
    return [pltpu.VMEM((LANES, seq), BF16), pltpu.VMEM((STAGE_SLOTS, seq, Q_TILE), F32),
            pltpu.VMEM((STAGE_SLOTS, seq, Q_TILE), BF16)]


def _diff_attn_kernel(seq, lam_init, q_ref, k_ref, v_ref, bias0_ref, bias1_ref, lq1, lk1, lq2, lk2,
                      subln_ref, o_ref, vt_scr, s_scr, p_scr):
    lane = lax.broadcasted_iota(jnp.int32, (Q_TILE, LANES), 1)
    lo = (lane < HEAD_DIM).astype(F32).astype(BF16)
    hi = (lane >= HEAD_DIM).astype(F32).astype(BF16)
    lam = (jnp.exp(jnp.sum(lq1[...] * lk1[...], axis=-1, keepdims=True))
           - jnp.exp(jnp.sum(lq2[...] * lk2[...], axis=-1, keepdims=True)) + lam_init)

    def q_of(rows, mp):
        return q_ref[rows, :] * (lo, hi)[mp]

    def k_of(keys, mp):
        return k_ref[keys, :]

    def bias_of(delta, mp):
        return (bias0_ref, bias1_ref)[mp][delta]

    def finish(rows, res):
        (a0, l0), (a1, l1) = res
        ot = a0 * (1.0 / l0) - a1 * (lam / l1)
        ms = jnp.mean(ot * ot, axis=0, keepdims=True)
        ot = ot * lax.rsqrt(ms + RMS_EPS) * (1.0 - lam_init)
        o_ref[rows, :] = (ot.T * subln_ref[...]).astype(o_ref.dtype)

    _causal_two_maps(seq, q_of, k_of, v_ref, bias_of, vt_scr, s_scr, p_scr, finish)


def _diff_attn(qkv, bias, lams, subln, lam_init, batch, seq):
    nh = DIFF_HEADS
    q3 = qkv.reshape(batch, seq, -1)
    ndelta = seq // Q_TILE
    vec = pl.BlockSpec((1, HEAD_DIM), lambda h, b: (0, 0))
    out = pl.pallas_call(
        functools.partial(_diff_attn_kernel, seq, lam_init),
        grid=(nh, batch),
        in_specs=[
            pl.BlockSpec((None, seq, LANES), lambda h, b: (b, 0, h)),
            pl.BlockSpec((None, seq, LANES), lambda h, b: (b, 0, nh + h)),
            pl.BlockSpec((None, seq, LANES), lambda h, b: (b, 0, 2 * nh + h)),
            pl.BlockSpec((None, ndelta, KV_CHUNK, Q_TILE), lambda h, b: (h, 0, 0, 0)),
            pl.BlockSpec((None, ndelta, KV_CHUNK, Q_TILE), lambda h, b: (nh + h, 0, 0, 0)),
            vec, vec, vec, vec,
            pl.BlockSpec((1, LANES), lambda h, b: (0, 0)),
        ],
        out_specs=pl.BlockSpec((None, seq, LANES), lambda h, b: (b, 0, h)),
        out_shape=jax.ShapeDtypeStruct((batch, seq, nh * LANES), BF16),
        scratch_shapes=_causal_scratch(seq),
        compiler_params=_params(("parallel", "parallel")),
    )(q3, q3, q3, bias, bias, *lams, subln)
    return out.reshape(batch * seq, nh * LANES)


def _mla_attn_kernel(seq, q_ref, k_ref, v_ref, mask_ref, o_ref, vt_scr, s_scr, p_scr):
    def q_of(rows, mp):
        return q_ref[rows, mp * LANES:(mp + 1) * LANES]

    def k_of(keys, mp):
        return k_ref[keys, mp * LANES:(mp + 1) * LANES]

    def bias_of(delta, mp):
        return mask_ref[...] if delta == 0 else None

    def finish(rows, res):
        (a0, l0), (a1, l1) = res
        first = lax.broadcasted_iota(jnp.int32, (LANES, Q_TILE), 0) < MLA_V
        ot = jnp.where(first, a0 * (1.0 / l0), a1 * (1.0 / l1))
        o_ref[rows, :] = ot.T.astype(o_ref.dtype)

    _causal_two_maps(seq, q_of, k_of, v_ref, bias_of, vt_scr, s_scr, p_scr, finish)


def _mla_attn(q, k, v, mask, batch, seq):
    npair = MLA_HEADS // 2
    out = pl.pallas_call(
        functools.partial(_mla_attn_kernel, seq),
        grid=(npair, batch),
        in_specs=[
            pl.BlockSpec((None, seq, 2 * LANES), lambda p, b: (b, 0, p)),
            pl.BlockSpec((None, seq, 2 * LANES), lambda p, b: (b, 0, p)),
            pl.BlockSpec((None, seq, LANES), lambda p, b: (b, 0, p)),
            pl.BlockSpec((KV_CHUNK, Q_TILE), lambda p, b: (0, 0)),
        ],
        out_specs=pl.BlockSpec((None, seq, LANES), lambda p, b: (b, 0, p)),
        out_shape=jax.ShapeDtypeStruct((batch, seq, MLA_HEADS * MLA_V), BF16),
        scratch_shapes=_causal_scratch(seq),
        compiler_params=_params(("parallel", "parallel")),
    )(q.reshape(batch, seq, -1), k.reshape(batch, seq, -1), v.reshape(batch, seq, -1), mask)
    return out.reshape(batch * seq, MLA_HEADS * MLA_V)


def _mla_proj_kernel(x_ref, g_ref, w_in_ref, qa_ref, kva_ref, wq_ref, wk_ref, wv_ref, ones_ref,
                     cosq_ref, sinq_ref, cosk_ref, sink_ref, q_out, k_out, v_out):
    hn = _rms(x_ref[...], g_ref[...]).astype(BF16)
    lat = _dot(hn, w_in_ref[...])
    cq = _rms(lat[:, :MLA_Q_RANK], qa_ref[...]).astype(BF16)
    ckv = _rms(lat[:, MLA_Q_RANK:MLA_Q_RANK + MLA_KV_RANK], kva_ref[...]).astype(BF16)
    kpe = lat[:, MLA_Q_RANK + MLA_KV_RANK:]
    kpe2 = jnp.concatenate([kpe, kpe], axis=1)
    v_out[...] = _dot(ckv, wv_ref[...]).astype(v_out.dtype)

    def heads(y, cos_ref, sin_ref):
        ms = _dot((y * y).astype(BF16), ones_ref[...]) * (1.0 / MLA_QK)
        t = y * cos_ref[...] + pltpu.roll(y, 2 * LANES - MLA_ROPE, 1) * sin_ref[...]
        return t * lax.rsqrt(ms + RMS_EPS)

    def finish(cols, yq, yk):
        q_out[:, cols] = heads(yq, cosq_ref, sinq_ref).astype(q_out.dtype)
        k_out[:, cols] = heads(yk + kpe2, cosk_ref, sink_ref).astype(k_out.dtype)

    pending = None
    for hp in range(MLA_HEADS // 2):
        cols = slice(hp * 2 * LANES, (hp + 1) * 2 * LANES)
        yq = _dot(cq, wq_ref[:, cols])
        yk = _dot(ckv, wk_ref[:, cols])
        if pending is not None:
            finish(*pending)
        pending = (cols, yq, yk)
    finish(*pending)


def _mla_proj(x2d, g, w_in, qa, kva, wq, wk, wv, ones, tables, seq):
    t = x2d.shape[0]
    per_seq = seq // ROW_TILE
    nq = MLA_HEADS * LANES
    tab = pl.BlockSpec((ROW_TILE, 2 * LANES), lambda i: (i % per_seq, 0))
    return pl.pallas_call(
        _mla_proj_kernel,
        grid=(t // ROW_TILE,),
        in_specs=[
            pl.BlockSpec((ROW_TILE, D_MODEL), lambda i: (i, 0)),
            _const_spec((1, D_MODEL)),
            _const_spec(w_in.shape),
            _const_spec((1, MLA_Q_RANK)),
            _const_spec((1, MLA_KV_RANK)),
            _const_spec(wq.shape),
            _const_spec(wk.shape),
            _const_spec(wv.shape),
            _const_spec((MXU_DIM, MXU_DIM)),
            tab, tab, tab, tab,
        ],
        out_specs=[pl.BlockSpec((ROW_TILE, nq), lambda i: (i, 0)),
                   pl.BlockSpec((ROW_TILE, nq), lambda i: (i, 0)),
                   pl.BlockSpec((ROW_TILE, MLA_HEADS * MLA_V), lambda i: (i, 0))],
        out_shape=[jax.ShapeDtypeStruct((t, nq), BF16), jax.ShapeDtypeStruct((t, nq), BF16),
                   jax.ShapeDtypeStruct((t, MLA_HEADS * MLA_V), BF16)],
        compiler_params=_params(("parallel",)),
    )(x2d, g, w_in, qa, kva, wq, wk, wv, ones, *tables)


def _ffn_kernel(tiles_per_seq, x_ref, o_ref, wo_ref, g_ref, wup_ref, cw_ref, cb_ref, wdn_ref,
                out_ref, h_scr, act_scr, conv_scr, tail_scr):
    rows = x_ref.shape[0]
    x2 = x_ref[...] + _dot(o_ref[...], wo_ref[...])
    out_ref[...] = x2
    h_scr[...] = _rms(x2, g_ref[...]).astype(BF16)
    first = pl.program_id(0) % tiles_per_seq == 0
    for c in range(D_FF // FF_CHUNK):
        cols = slice(c * FF_CHUNK, (c + 1) * FF_CHUNK)
        gate = _dot(h_scr[...], wup_ref[:, cols])
        up = _dot(h_scr[...], wup_ref[:, D_FF + c * FF_CHUNK:D_FF + (c + 1) * FF_CHUNK])
        conv_scr[0:SUBLANES, :] = jnp.where(first, 0.0, tail_scr[:, cols])
        conv_scr[SUBLANES:, :] = gate
        tail_scr[:, cols] = gate[rows - SUBLANES:, :]
        conv = cb_ref[:, cols] + cw_ref[CONV_WIDTH - 1:CONV_WIDTH, cols] * gate
        for j in range(CONV_WIDTH - 1):
            back = CONV_WIDTH - 1 - j
            conv = conv + cw_ref[j:j + 1, cols] * conv_scr[SUBLANES - back:SUBLANES - back + rows, :]
        act_scr[:, cols] = (conv / (1.0 + jnp.exp(-conv)) * up).astype(BF16)
    out_ref[...] += _dot(act_scr[...], wdn_ref[...])


def _ffn(x2d, o, wo, g, wup, cw, cb, wdn, seq):
    t = x2d.shape[0]
    ko = o.shape[1]
    return pl.pallas_call(
        functools.partial(_ffn_kernel, seq // ROW_TILE),
        grid=(t // ROW_TILE,),
        in_specs=[
            pl.BlockSpec((ROW_TILE, D_MODEL), lambda i: (i, 0)),
            pl.BlockSpec((ROW_TILE, ko), lambda i: (i, 0)),
            _const_spec((ko, D_MODEL)),
            _const_spec((1, D_MODEL)),
            _const_spec((D_MODEL, 2 * D_FF)),
            _const_spec((CONV_WIDTH, D_FF)),
            _const_spec((1, D_FF)),
            _const_spec((D_FF, D_MODEL)),
        ],
        out_specs=pl.BlockSpec((ROW_TILE, D_MODEL), lambda i: (i, 0)),
        out_shape=jax.ShapeDtypeStruct((t, D_MODEL), F32),
        scratch_shapes=[
            pltpu.VMEM((ROW_TILE, D_MODEL), BF16),
            pltpu.VMEM((ROW_TILE, D_FF), BF16),
            pltpu.VMEM((ROW_TILE + SUBLANES, FF_CHUNK), F32),
            pltpu.VMEM((SUBLANES, D_FF), F32),
        ],
        compiler_params=_params(("arbitrary",)),
    )(x2d, o, wo, g, wup, cw, cb, wdn)


def _bucket_of(dist):
    max_exact = NUM_BUCKETS // 2
    if dist < max_exact:
        return dist
    large = max_exact + int(math.log(dist / max_exact) / math.log(MAX_DISTANCE / max_exact)
                            * (NUM_BUCKETS - max_exact))
    return min(large, NUM_BUCKETS - 1)


def _bias_kernel(tiles, shape, keys_first, tab_ref, out_ref):
    col = pl.program_id(0)
    max_exact = NUM_BUCKETS // 2
    q_axis, k_axis = (1, 0) if keys_first else (0, 1)
    base = (lax.broadcasted_iota(jnp.int32, shape, q_axis)
            - lax.broadcasted_iota(jnp.int32, shape, k_axis))
    for t, (offset, dilation, window, first_key) in enumerate(tiles):
        off = base + offset
        dist = jnp.maximum(off, 0) * dilation
        d_f = jnp.maximum(dist, 1).astype(F32)
        large = max_exact + (jnp.log(d_f * (1.0 / max_exact))
                             * ((NUM_BUCKETS - max_exact) / math.log(MAX_DISTANCE / max_exact))
                             ).astype(jnp.int32)
        bucket = jnp.where(dist < max_exact, dist, jnp.minimum(large, NUM_BUCKETS - 1))
        near = max(offset - (shape[k_axis] - 1), 0) * dilation
        far = max(offset + shape[q_axis] - 1, 0) * dilation
        val = jnp.zeros(shape, F32)
        for b in range(max(_bucket_of(near) - 1, 0), min(_bucket_of(far) + 1, NUM_BUCKETS - 1) + 1):
            val = jnp.where(bucket == b, tab_ref[b, col], val)
        val = jnp.where(off >= 0, val * LOG2E, NEG)
        if window is not None:
            val = jnp.where(off <= window, val, NEG)
        if first_key:
            val = jnp.where(lax.broadcasted_iota(jnp.int32, shape, k_axis) >= first_key, val, NEG)
        out_ref[t] = val


def _bias_tiles(table, ncols, tiles, shape, keys_first=False):
    return pl.pallas_call(
        functools.partial(_bias_kernel, tiles, shape, keys_first),
        grid=(ncols,),
        in_specs=[pl.BlockSpec(memory_space=pltpu.SMEM)],
        out_specs=pl.BlockSpec((None, len(tiles)) + shape, lambda c: (c, 0, 0, 0)),
        out_shape=jax.ShapeDtypeStruct((ncols, len(tiles)) + shape, F32),
        compiler_params=_params(("parallel",)),
    )(table)


def _band_tiles(dilation, window):
    return ((BLOCK, dilation, window, 0), (0, dilation, window, 0),
            (BLOCK, dilation, window, BLOCK))


BAND_PREV, BAND_NEXT, BAND_LAST = range(3)


def _band_keys(u, starts_sequence, nblk):
    if not starts_sequence:
        return (u - 1) * BLOCK, BAND_PREV
    if u + 1 < nblk:
        return u * BLOCK, BAND_NEXT
    return (u - 1) * BLOCK, BAND_LAST


def _causal_mask():
    dist = jnp.arange(Q_TILE)[None, :] - jnp.arange(KV_CHUNK)[:, None]
    return jnp.where(dist >= 0, 0.0, NEG).astype(F32)


def _seg_ones():
    i = jnp.arange(MXU_DIM)
    return ((i[:, None] // HEAD_DIM == i[None, :] // HEAD_DIM).astype(F32) / HEAD_DIM).astype(BF16)


def kernel(x, rel_bias_table, norm_mix, norm_ffn, a_w_in, a_q_norm, a_k_norm, a_w_out, b_w_in, b_q_a_norm, b_kv_a_norm, b_w_q_up, b_w_kv_up, b_q_norm, b_k_norm, b_w_out, c_w_in, c_q_norm, c_k_norm, c_lambda_q1, c_lambda_k1, c_lambda_q2, c_lambda_k2, c_subln, c_w_out, d_w_in, d_q_norm, d_k_norm, d_sinks, d_w_out, f_w_up, f_conv_w, f_conv_b, f_w_down):
    batch, seq, _ = x.shape
    depth = norm_mix.shape[0]
    assert seq == MAX_DISTANCE and seq % ROW_TILE == 0
    table = rel_bias_table.astype(F32)
    bd = _seg_ones()
    qk_scale = HEAD_DIM ** -0.5 * LOG2E
    row = lambda v: v.reshape(1, -1).astype(F32)
    xf = x.reshape(batch * seq, D_MODEL)

    for i in range(depth):
        m, j = i % 4, i // 4
        g_mix = row(norm_mix[i])
        if m == 0:
            hw = A_HEADS * HEAD_DIM
            gain = jnp.concatenate(
                [jnp.concatenate([jnp.tile(a_q_norm[j, g], A_HEADS) * qk_scale,
                                  jnp.tile(a_k_norm[j, g], A_HEADS),
                                  jnp.ones((hw,), F32)]) for g in range(N_DIL)])
            segs = []
            for g in range(N_DIL):
                for t in range(3):
                    for half in range(hw // MXU_DIM):
                        c0 = (g * 3 + t) * hw + half * MXU_DIM
                        segs.append((c0, MXU_DIM, MXU_DIM if t < 2 else 0, g, c0 - g * 3 * hw))
            qkvs = _proj(xf, g_mix, a_w_in[j].astype(BF16), row(gain), bd, tuple(segs),
                         tuple((3 * hw, dil) for _, dil in DIL_GROUPS), seq)
            bias = _bias_tiles(table, A_HEADS,
                               tuple(_band_tiles(dil, window // dil)[tile] for tile in range(3)
                                     for window, dil in DIL_GROUPS),
                               (BLOCK, 2 * BLOCK))
            o = _dil_attn(qkvs, bias, batch, seq)
            w_out = a_w_out[j]
        elif m == 1:
            w_in = b_w_in[j]
            lat_w = MLA_Q_RANK + MLA_KV_RANK
            pe = w_in[:, lat_w:]
            half = MLA_ROPE // 2
            swap = lambda a: jnp.concatenate([a[..., half:], a[..., :half]], axis=-1)
            w_in_ext = jnp.concatenate(
                [w_in[:, :lat_w], jnp.zeros((D_MODEL, MLA_NOPE), F32), pe, swap(pe)], axis=1)
            wq = b_w_q_up[j].reshape(MLA_Q_RANK, MLA_HEADS, MLA_QK)
            wq = jnp.concatenate([wq, swap(wq[..., MLA_NOPE:])], axis=-1).reshape(MLA_Q_RANK, -1)
            wkv = b_w_kv_up[j].reshape(MLA_KV_RANK, MLA_HEADS, MLA_NOPE + MLA_V)
            wk = jnp.concatenate([wkv[..., :MLA_NOPE], jnp.zeros_like(wkv[..., :MLA_NOPE])],
                                 axis=-1).reshape(MLA_KV_RANK, -1)
            wv = wkv[..., MLA_NOPE:].reshape(MLA_KV_RANK, -1)
            inv_freq = ROPE_THETA ** (-jnp.arange(0, MLA_ROPE, 2, dtype=F32) / MLA_ROPE)
            ang = jnp.arange(seq, dtype=F32)[:, None] * inv_freq[None, :]
            cos, sin = jnp.cos(ang), jnp.sin(ang)
            ones = jnp.ones((seq, MLA_NOPE), F32)
            zeros = jnp.zeros((seq, MLA_ROPE), F32)
            cos_t = jnp.concatenate([ones, cos, cos, zeros], axis=1)
            sin_t = jnp.concatenate([0.0 * ones, -sin, sin, zeros], axis=1)

            def rope_tables(gn, scale):
                gpad = jnp.concatenate([gn, swap(gn[MLA_NOPE:])]).astype(F32)
                cos_g = cos_t * gpad[None, :] * scale
                sin_g = sin_t * jnp.roll(gpad, -MLA_ROPE)[None, :] * scale
                return jnp.tile(cos_g, (1, 2)), jnp.tile(sin_g, (1, 2))

            lane = jnp.arange(MXU_DIM)
            real_sum = ((lane[:, None] // LANES == lane[None, :] // LANES)
                        & (lane[:, None] % LANES < MLA_QK)).astype(BF16)
            tables = (rope_tables(b_q_norm[j], MLA_QK ** -0.5 * LOG2E)
                      + rope_tables(b_k_norm[j], 1.0))
            q, k, v = _mla_proj(xf, g_mix, w_in_ext.astype(BF16), row(b_q_a_norm[j]),
                                row(b_kv_a_norm[j]), wq.astype(BF16), wk.astype(BF16),
                                wv.astype(BF16), real_sum, tables, seq)
            o = _mla_attn(q, k, v, _causal_mask(), batch, seq)
            w_out = b_w_out[j]
        elif m == 2:
            qk_w = DIFF_HEADS * 2 * HEAD_DIM
            gain = jnp.concatenate([jnp.tile(c_q_norm[j], 2 * DIFF_HEADS) * qk_scale,
                                    jnp.tile(c_k_norm[j], 2 * DIFF_HEADS), jnp.ones((qk_w,), F32)])
            segs = tuple((c0, MXU_DIM, MXU_DIM if c0 < 2 * qk_w else 0, 0, c0)
                         for c0 in range(0, 3 * qk_w, MXU_DIM))
            (qkv,) = _proj(xf, g_mix, c_w_in[j].astype(BF16), row(gain), bd, segs,
                           ((3 * qk_w, 1),), seq)
            bias = _bias_tiles(table, 2 * DIFF_HEADS,
                               tuple((d * Q_TILE, 1, None, 0) for d in range(seq // Q_TILE)),
                               (KV_CHUNK, Q_TILE), keys_first=True)
            lam_init = 0.8 - 0.6 * math.exp(-0.3 * i)
            lams = [row(v) for v in (c_lambda_q1[j], c_lambda_k1[j], c_lambda_q2[j], c_lambda_k2[j])]
            o = _diff_attn(qkv, bias, lams, row(c_subln[j]), lam_init, batch, seq)
            w_out = c_w_out[j]
        else:
            npair = SWA_Q_HEADS // 2
            qw = SWA_Q_HEADS * HEAD_DIM
            order = jnp.arange(SWA_Q_HEADS).reshape(2, npair).T.reshape(-1)
            w_in = d_w_in[j]
            wq = w_in[:, :qw].reshape(D_MODEL, SWA_Q_HEADS, HEAD_DIM)[:, order].reshape(D_MODEL, qw)
            w_perm = jnp.concatenate([wq, w_in[:, qw:]], axis=1)
            kvw = SWA_KV_HEADS * HEAD_DIM
            gain = jnp.concatenate([jnp.tile(d_q_norm[j], SWA_Q_HEADS) * qk_scale,
                                    jnp.tile(d_k_norm[j], SWA_KV_HEADS), jnp.ones((kvw,), F32)])
            segs = tuple((c0, MXU_DIM, MXU_DIM, 0, c0) for c0 in range(0, qw, MXU_DIM))
            segs += ((qw, 2 * kvw, kvw, 0, qw),)
            (qkv,) = _proj(xf, g_mix, w_perm.astype(BF16), row(gain), bd, segs,
                           ((qw + 2 * kvw, 1),), seq)
            bias = _bias_tiles(table, SWA_Q_HEADS, _band_tiles(1, SWA_WINDOW - 1),
                               (BLOCK, 2 * BLOCK))
            o = _swa_attn(qkv, bias, d_sinks[j].astype(F32), batch, seq)
            w_out = d_w_out[j].reshape(SWA_Q_HEADS, HEAD_DIM, D_MODEL)[order].reshape(qw, D_MODEL)
        xf = _ffn(xf, o, w_out.astype(BF16), row(norm_ffn[i]), f_w_up[i].astype(BF16),
                  f_conv_w[i].astype(F32), row(f_conv_b[i]), f_w_down[i].astype(BF16), seq)
    return xf.reshape(batch, seq, D_MODEL)
```
